```python
import jax, jax.numpy as jnp
from jax import lax
import numpy as np

D_MODEL = 1024
BATCH = 32
SEQ = 2048
DEPTH = 2
DEC_BATCH = 32
DEC_SEQ = 16
PAST_LEN = 2048

CHUNK = 64
N_HEADS = 16
HEAD_DIM = D_MODEL // N_HEADS
D_FF = 2816
CONV_K = 31
N_A = DEPTH // 2
N_B = DEPTH - N_A
N_SUB = 3
ALPHA = float((2 * DEPTH) ** 0.25)
BETA = float((8 * DEPTH) ** -0.25)
Q_BLOCK = 128
LN_EPS = 1e-5

kernel_name = 'yoco_conformer_fox_stream_step'


def _layer_norm(x, g, b):
    xf = x.astype(jnp.float32)
    mu = jnp.mean(xf, axis=-1, keepdims=True)
    var = jnp.mean(jnp.square(xf - mu), axis=-1, keepdims=True)
    y = (xf - mu) * lax.rsqrt(var + LN_EPS) * g.astype(jnp.float32) + b.astype(jnp.float32)
    return y.astype(x.dtype)


def _modulate(x, shift, scale):
    return x * (1 + scale[:, None, :]) + shift[:, None, :]


def _swiglu(h, w_gate, w_up, w_down):
    return (jax.nn.silu(h @ w_gate) * (h @ w_up)) @ w_down


def _conv_module(h, prev, pw1_w, pw1_b, dw_w, dw_b, cln_g, cln_b, pw2_w, pw2_b):
    u = h @ pw1_w + pw1_b
    u = u[..., :D_MODEL] * jax.nn.sigmoid(u[..., D_MODEL:])
    u_full = jnp.concatenate([prev.astype(u.dtype), u], axis=1)
    z = lax.conv_general_dilated(
        u_full, dw_w.astype(u.dtype)[:, None, :], window_strides=(1,), padding='VALID',
        dimension_numbers=('NWC', 'WIO', 'NWC'), feature_group_count=D_MODEL) + dw_b
    z = jax.nn.silu(_layer_norm(z, cln_g, cln_b))
    return z @ pw2_w + pw2_b, u_full[:, -(CONV_K - 1):]


def _fox_attention(q, k, v, logf):
    tq, tk = q.shape[1], k.shape[1]
    off = tk - tq
    cum = jnp.cumsum(logf.astype(jnp.float32), axis=1).transpose(0, 2, 1)
    qb = min(Q_BLOCK, tq)
    scale = HEAD_DIM ** -0.5
    outs = []
    for start in range(0, tq, qb):
        stop = min(start + qb, tq)
        kend = off + stop
        s = jnp.einsum('bqhd,bkhd->bhqk', q[:, start:stop], k[:, :kend]).astype(jnp.float32) * scale
        bias = cum[:, :, off + start:off + stop, None] - cum[:, :, None, :kend]
        qpos = off + start + jnp.arange(stop - start)
        kpos = jnp.arange(kend)
        mask = kpos[None, :] <= qpos[:, None]
        p = jax.nn.softmax(jnp.where(mask, s + bias, -jnp.inf), axis=-1)
        outs.append(jnp.einsum('bhqk,bkhd->bqhd', p.astype(v.dtype), v[:, :kend]))
    return jnp.concatenate(outs, axis=1)


def _trunk(x, c, conv_prev, kv_cache, w_ada, b_ada, ln_g, ln_b, ffn_w_gate, ffn_w_up, ffn_w_down,
           pw1_w, pw1_b, dw_w, dw_b, conv_ln_g, conv_ln_b, pw2_w, pw2_b,
           w_ada_kv, b_ada_kv, w_k, w_v, w_f, b_f, w_q, w_o):
    bsz, t, _ = x.shape
    c_act = jax.nn.silu(c)
    new_conv = []
    for l in range(DEPTH):
        if l == N_A:
            shift_kv, scale_kv = jnp.split(c_act @ w_ada_kv + b_ada_kv, 2, axis=-1)
            h_kv = _modulate(x, shift_kv, scale_kv)
            k_new = (h_kv @ w_k).reshape(bsz, t, N_HEADS, HEAD_DIM)
            v_new = (h_kv @ w_v).reshape(bsz, t, N_HEADS, HEAD_DIM)
            logf_new = jax.nn.log_sigmoid((h_kv @ w_f + b_f).astype(jnp.float32))
            if kv_cache is None:
                k_all, v_all, logf_all = k_new, v_new, logf_new
            else:
                ck, cv, clf = kv_cache
                k_all = jnp.concatenate([ck.astype(k_new.dtype), k_new], axis=1)
                v_all = jnp.concatenate([cv.astype(v_new.dtype), v_new], axis=1)
                logf_all = jnp.concatenate([clf.astype(jnp.float32), logf_new], axis=1)
        mods = (c_act @ w_ada[l] + b_ada[l]).reshape(bsz, N_SUB, 3, D_MODEL)
        shift, scale, gate = mods[:, :, 0], mods[:, :, 1], 1 + mods[:, :, 2]
        h = _modulate(x, shift[:, 0], scale[:, 0])
        f1 = _swiglu(h, ffn_w_gate[l, 0], ffn_w_up[l, 0], ffn_w_down[l, 0])
        x = _layer_norm(ALPHA * x + 0.5 * gate[:, 0, None, :] * f1, ln_g[l, 0], ln_b[l, 0])
        h = _modulate(x, shift[:, 1], scale[:, 1])
        if l < N_A:
            mix, st = _conv_module(h, conv_prev[l], pw1_w[l], pw1_b[l], dw_w[l], dw_b[l],
                                   conv_ln_g[l], conv_ln_b[l], pw2_w[l], pw2_b[l])
            new_conv.append(st)
        else:
            j = l - N_A
            q = (h @ w_q[j]).reshape(bsz, t, N_HEADS, HEAD_DIM)
            mix = _fox_attention(q, k_all, v_all, logf_all).reshape(bsz, t, D_MODEL) @ w_o[j]
        x = _layer_norm(ALPHA * x + gate[:, 1, None, :] * mix, ln_g[l, 1], ln_b[l, 1])
        h = _modulate(x, shift[:, 2], scale[:, 2])
        f2 = _swiglu(h, ffn_w_gate[l, 1], ffn_w_up[l, 1], ffn_w_down[l, 1])
        x = _layer_norm(ALPHA * x + 0.5 * gate[:, 2, None, :] * f2, ln_g[l, 2], ln_b[l, 2])
    return x, jnp.stack(new_conv), k_new, v_new, logf_new


def setup_inputs(seed: int = 0) -> dict:
    key = jax.random.key(seed)
    ks = jax.random.split(key, 40)
    n = lambda i, shape, s: jax.random.normal(ks[i], shape, jnp.float32) * s
    D = D_MODEL
    return {
        'x_prompt': n(0, (BATCH, SEQ, D), 1.0),
        'x_sample': n(1, (DEC_BATCH, DEC_SEQ, D), 1.0),
        'cache_conv': n(2, (N_A, DEC_BATCH, CONV_K - 1, D), 0.5),
        'cache_k': n(3, (DEC_BATCH, PAST_LEN, N_HEADS, HEAD_DIM), 1.0),
        'cache_v': n(4, (DEC_BATCH, PAST_LEN, N_HEADS, HEAD_DIM), BETA),
        'cache_logf': jax.nn.log_sigmoid(3.0 + n(5, (DEC_BATCH, PAST_LEN, N_HEADS), 1.0)),
        'c_prompt': n(6, (BATCH, D), 1.0),
        'c_sample': n(7, (DEC_BATCH, D), 1.0),
        'w_ada': n(8, (DEPTH, D, N_SUB * 3 * D), 0.1 * D ** -0.5),
        'b_ada': n(9, (DEPTH, N_SUB * 3 * D), 0.01),
        'ln_g': 1.0 + n(10, (DEPTH, N_SUB, D), 0.01),
        'ln_b': n(11, (DEPTH, N_SUB, D), 0.01),
        'ffn_w_gate': n(12, (DEPTH, 2, D, D_FF), D ** -0.5),
        'ffn_w_up': n(13, (DEPTH, 2, D, D_FF), D ** -0.5),
        'ffn_w_down': n(14, (DEPTH, 2, D_FF, D), BETA * D_FF ** -0.5),
        'pw1_w': n(15, (N_A, D, 2 * D), D ** -0.5),
        'pw1_b': n(16, (N_A, 2 * D), 0.01),
        'dw_w': n(17, (N_A, CONV_K, D), CONV_K ** -0.5),
        'dw_b': n(18, (N_A, D), 0.01),
        'conv_ln_g': 1.0 + n(19, (N_A, D), 0.01),
        'conv_ln_b': n(20, (N_A, D), 0.01),
        'pw2_w': n(21, (N_A, D, D), BETA * D ** -0.5),
        'pw2_b': n(22, (N_A, D), 0.01),
        'w_ada_kv': n(23, (D, 2 * D), 0.1 * D ** -0.5),
        'b_ada_kv': n(24, (2 * D,), 0.01),
        'w_k': n(25, (D, D), D ** -0.5),
        'w_v': n(26, (D, D), BETA * D ** -0.5),
        'w_f': n(27, (D, N_HEADS), D ** -0.5),
        'b_f': jnp.linspace(1.0, 6.0, N_HEADS, dtype=jnp.float32) + n(28, (N_HEADS,), 0.01),
        'w_q': n(29, (N_B, D, D), D ** -0.5),
        'w_o': n(30, (N_B, D, D), BETA * D ** -0.5),
    }


def reference(x_prompt, x_sample, cache_conv, cache_k, cache_v, cache_logf, c_prompt, c_sample,
              w_ada, b_ada, ln_g, ln_b, ffn_w_gate, ffn_w_up, ffn_w_down,
              pw1_w, pw1_b, dw_w, dw_b, conv_ln_g, conv_ln_b, pw2_w, pw2_b,
              w_ada_kv, b_ada_kv, w_k, w_v, w_f, b_f, w_q, w_o):
    weights = (w_ada, b_ada, ln_g, ln_b, ffn_w_gate, ffn_w_up, ffn_w_down,
               pw1_w, pw1_b, dw_w, dw_b, conv_ln_g, conv_ln_b, pw2_w, pw2_b,
               w_ada_kv, b_ada_kv, w_k, w_v, w_f, b_f, w_q, w_o)
    zero_conv = jnp.zeros((N_A, x_prompt.shape[0], CONV_K - 1, D_MODEL), x_prompt.dtype)
    y_prompt, conv_p, k_p, v_p, logf_p = _trunk(x_prompt, c_prompt, zero_conv, None, *weights)
    y_sample, conv_s, k_s, v_s, logf_s = _trunk(
        x_sample, c_sample, cache_conv, (cache_k, cache_v, cache_logf), *weights)
    return (y_prompt, y_sample, conv_p, k_p, v_p, logf_p, conv_s, k_s, v_s, logf_s)
```

```python
import functools

import jax
import jax.numpy as jnp
from jax import lax
from jax.experimental import pallas as pl
from jax.experimental.pallas import tpu as pltpu

F32 = jnp.float32
BF16 = jnp.bfloat16

LN_EPS = 1e-5
LANES = 128
SUBLANES = 8
HEAD_PAIR = 2
CONV_HALO = 32
VMEM_LIMIT_BYTES = 56 * 1024 * 1024
NEG_BIG = -1e30


def _cparams(n_grid):
    return pltpu.CompilerParams(
        dimension_semantics=("arbitrary",) * n_grid,
        vmem_limit_bytes=VMEM_LIMIT_BYTES,
    )


def _resident(shape, index_map):
    return pl.BlockSpec(shape, index_map, pipeline_mode=pl.Buffered(1))


def _layer_norm(y, g, b):
    mu = jnp.mean(y, axis=-1, keepdims=True)
    yc = y - mu
    var = jnp.mean(yc * yc, axis=-1, keepdims=True)
    return yc * lax.rsqrt(var + LN_EPS) * g + b


def _dot(a, b):
    return jnp.dot(a, b, preferred_element_type=F32)


def _dot_nt(a, b):
    return lax.dot_general(a, b, (((1,), (1,)), ((), ())), preferred_element_type=F32)


def _split3(x):
    hi = x.astype(BF16)
    r1 = x - hi.astype(F32)
    mid = r1.astype(BF16)
    lo = (r1 - mid.astype(F32)).astype(BF16)
    return hi, mid, lo


def _ada_kernel(c_ref, w_ref, b_ref, o_ref):
    c = c_ref[...]
    ca = (c * jax.nn.sigmoid(c)).astype(BF16)
    o_ref[...] = _dot(ca, w_ref[...].astype(BF16)) + b_ref[...]


def _ada(c, w, b):
    bc, d = c.shape
    nl, _, n = w.shape
    nch = n // d
    out = pl.pallas_call(
        _ada_kernel,
        grid=(nl, nch),
        in_specs=[
            _resident((bc, d), lambda l, j: (0, 0)),
            pl.BlockSpec((None, d, d), lambda l, j: (l, 0, j)),
            pl.BlockSpec((None, 1, d), lambda l, j: (l * nch + j, 0, 0)),
        ],
        out_specs=pl.BlockSpec((None, bc, d), lambda l, j: (l * nch + j, 0, 0)),
        out_shape=jax.ShapeDtypeStruct((nl * nch, bc, d), F32),
        compiler_params=_cparams(2),
        name="ada",
    )(c, w, b.reshape(nl * nch, 1, d))
    return out.reshape(nl * nch, bc, 1, d)


def _mod_spec(bb, d, chunk):
    return pl.BlockSpec((None, bb, 1, d), lambda b, t: (chunk, b, 0, 0))


def _ffn_kernel(x_ref, sh_ref, sc_ref, gt_ref, wg_ref, wu_ref, wd_ref, g_ref, b_ref, o_ref,
                *, alpha, ff_chunk):
    bb, tt, d = x_ref.shape
    dff = wg_ref.shape[1]
    x = x_ref[...]
    h = (x * (1.0 + sc_ref[...]) + sh_ref[...]).reshape(bb * tt, d).astype(BF16)
    acc = None
    for c0 in range(0, dff, ff_chunk):
        c1 = min(c0 + ff_chunk, dff)
        g = _dot(h, wg_ref[:, c0:c1])
        u = _dot(h, wu_ref[:, c0:c1])
        a = (g * jax.nn.sigmoid(g) * u).astype(BF16)
        part = _dot(a, wd_ref[c0:c1, :])
        acc = part if acc is None else acc + part
    f = acc.reshape(bb, tt, d)
    y = alpha * x + (0.5 * (1.0 + gt_ref[...])) * f
    o_ref[...] = _layer_norm(y, g_ref[...], b_ref[...])


def _ffn(x, mods, chunk0, wg, wu, wd, widx, ln_g, ln_b, lnidx, *, alpha, bb, tt):
    b, t, d = x.shape
    dff = wg.shape[-1]
    l, i = widx
    kern = functools.partial(_ffn_kernel, alpha=alpha, ff_chunk=512)
    return pl.pallas_call(
        kern,
        grid=(b // bb, t // tt),
        in_specs=[
            pl.BlockSpec((bb, tt, d), lambda bi, ti: (bi, ti, 0)),
            _mod_spec(bb, d, chunk0 + 0),
            _mod_spec(bb, d, chunk0 + 1),
            _mod_spec(bb, d, chunk0 + 2),
            _resident((None, None, d, dff), lambda bi, ti: (l, i, 0, 0)),
            _resident((None, None, d, dff), lambda bi, ti: (l, i, 0, 0)),
            _resident((None, None, dff, d), lambda bi, ti: (l, i, 0, 0)),
            _resident((None, 1, d), lambda bi, ti: (lnidx, 0, 0)),
            _resident((None, 1, d), lambda bi, ti: (lnidx, 0, 0)),
        ],
        out_specs=pl.BlockSpec((bb, tt, d), lambda bi, ti: (bi, ti, 0)),
        out_shape=jax.ShapeDtypeStruct((b, t, d), F32),
        compiler_params=_cparams(2),
        name="ffn",
    )(x, mods, mods, mods, wg, wu, wd, ln_g, ln_b)


def _conv_kernel(x_ref, sh_ref, sc_ref, gt_ref, prev_ref, w1_ref, b1_ref, dw_ref, dwb_ref,
                 cg_ref, cb_ref, w2_ref, b2_ref, g_ref, b_ref, o_ref, st_ref, u_scr, z_scr,
                 *, alpha, conv_k, strip):
    bb, tt, d = x_ref.shape
    ti = pl.program_id(1)
    keep = conv_k - 1
    off = CONV_HALO - keep

    @pl.when(ti == 0)
    def _():
        u_scr[:, 0:off, :] = jnp.zeros((bb, off, d), F32)
        u_scr[:, off:CONV_HALO, :] = prev_ref[...]

    x = x_ref[...]
    h = (x * (1.0 + sc_ref[...]) + sh_ref[...]).reshape(bb * tt, d).astype(BF16)
    u2 = _dot(h, w1_ref[...]) + b1_ref[...]
    u = u2[:, :d] * jax.nn.sigmoid(u2[:, d:])
    u_scr[:, CONV_HALO:CONV_HALO + tt, :] = u.reshape(bb, tt, d)

    n_strips = tt // strip

    def strip_body(it, carry):
        bi = it // n_strips
        r0 = pl.multiple_of((it % n_strips) * strip, strip)
        for j in range(d // LANES):
            lanes = pl.ds(j * LANES, LANES)
            win = u_scr[bi, pl.ds(r0, strip + CONV_HALO), lanes]
            acc = jnp.zeros((strip, LANES), F32)
            for r in range(SUBLANES):
                taps = [k for k in range(conv_k) if (off + k) % SUBLANES == r]
                if not taps:
                    continue
                shifted = win[r:r + strip + CONV_HALO - SUBLANES] if r else win
                for k in taps:
                    a0 = off + k - r
                    acc = acc + dw_ref[k:k + 1, lanes] * shifted[a0:a0 + strip]
            z_scr[bi, pl.ds(r0, strip), lanes] = acc
        return carry

    lax.fori_loop(0, bb * n_strips, strip_body, 0)

    st_ref[...] = u_scr[:, tt + off:tt + CONV_HALO, :]
    if tt >= CONV_HALO:
        u_scr[:, 0:CONV_HALO, :] = u_scr[:, tt:tt + CONV_HALO, :]

    z = z_scr[...] + dwb_ref[...]
    z = _layer_norm(z, cg_ref[...], cb_ref[...])
    z = (z * jax.nn.sigmoid(z)).reshape(bb * tt, d).astype(BF16)
    mix = (_dot(z, w2_ref[...]) + b2_ref[...]).reshape(bb, tt, d)
    y = alpha * x + (1.0 + gt_ref[...]) * mix
    o_ref[...] = _layer_norm(y, g_ref[...], b_ref[...])


def _conv(x, mods, chunk0, prev, w1, b1, dw, dwb, cg, cb, w2, b2, ln_g, ln_b, lnidx,
          *, alpha, bb, tt):
    b, t, d = x.shape
    conv_k = dw.shape[0]
    keep = conv_k - 1
    assert keep <= CONV_HALO and (t == tt or tt >= CONV_HALO)
    strip = min(tt, 64)
    kern = functools.partial(_conv_kernel, alpha=alpha, conv_k=conv_k, strip=strip)
    vec = lambda: _resident((1, d), lambda bi, ti: (0, 0))
    return pl.pallas_call(
        kern,
        grid=(b // bb, t // tt),
        in_specs=[
            pl.BlockSpec((bb, tt, d), lambda bi, ti: (bi, ti, 0)),
            _mod_spec(bb, d, chunk0 + 0),
            _mod_spec(bb, d, chunk0 + 1),
            _mod_spec(bb, d, chunk0 + 2),
            pl.BlockSpec((bb, keep, d), lambda bi, ti: (bi, 0, 0)),
            _resident((d, 2 * d), lambda bi, ti: (0, 0)),
            _resident((1, 2 * d), lambda bi, ti: (0, 0)),
            _resident((conv_k, d), lambda bi, ti: (0, 0)),
            vec(), vec(), vec(),
            _resident((d, d), lambda bi, ti: (0, 0)),
            vec(),
            _resident((None, 1, d), lambda bi, ti: (lnidx, 0, 0)),
            _resident((None, 1, d), lambda bi, ti: (lnidx, 0, 0)),
        ],
        out_specs=[
            pl.BlockSpec((bb, tt, d), lambda bi, ti: (bi, ti, 0)),
            pl.BlockSpec((bb, keep, d), lambda bi, ti: (bi, 0, 0)),
        ],
        out_shape=[
            jax.ShapeDtypeStruct((b, t, d), F32),
            jax.ShapeDtypeStruct((b, keep, d), F32),
        ],
        scratch_shapes=[
            pltpu.VMEM((bb, CONV_HALO + tt, d), F32),
            pltpu.VMEM((bb, tt, d), F32),
        ],
        compiler_params=_cparams(2),
        name="conv",
    )(x, mods, mods, mods, prev, w1, b1, dw, dwb, cg, cb, w2, b2, ln_g, ln_b)


def _log_sigmoid(f):
    return jnp.minimum(f, 0.0) - jnp.log1p(jnp.exp(-jnp.abs(f)))


def _kv_kernel(x_ref, sh_ref, sc_ref, wkv_ref, wf_ref, bf_ref,
               k_ref, v_ref, lf_ref, lfp_ref, kb_ref, vb_ref):
    bb, tt, d = x_ref.shape
    nh = lf_ref.shape[-1]
    x = x_ref[...]
    h = (x * (1.0 + sc_ref[...]) + sh_ref[...]).reshape(bb * tt, d).astype(BF16)
    kv = _dot(h, wkv_ref[...])
    k = kv[:, :d].reshape(bb, tt, d)
    v = kv[:, d:].reshape(bb, tt, d)
    k_ref[...] = k
    v_ref[...] = v
    kb_ref[...] = k.astype(BF16)
    vb_ref[...] = v.astype(BF16)
    lf = _log_sigmoid(_dot(h, wf_ref[...]) + bf_ref[...])
    lfp_ref[...] = lf.reshape(bb, tt, LANES)
    lf_ref[...] = lf[:, :nh].reshape(bb, tt, nh)


def _kv(x, modskv, wkv, wf, bf, nh, *, bb, tt):
    b, t, d = x.shape
    blk = lambda w: pl.BlockSpec((bb, tt, w), lambda bi, ti: (bi, ti, 0))
    return pl.pallas_call(
        _kv_kernel,
        grid=(b // bb, t // tt),
        in_specs=[
            blk(d),
            _mod_spec(bb, d, 0),
            _mod_spec(bb, d, 1),
            _resident((d, 2 * d), lambda bi, ti: (0, 0)),
            _resident((d, LANES), lambda bi, ti: (0, 0)),
            _resident((1, LANES), lambda bi, ti: (0, 0)),
        ],
        out_specs=[blk(d), blk(d), blk(nh), blk(LANES), blk(d), blk(d)],
        out_shape=[
            jax.ShapeDtypeStruct((b, t, d), F32),
            jax.ShapeDtypeStruct((b, t, d), F32),
            jax.ShapeDtypeStruct((b, t, nh), F32),
            jax.ShapeDtypeStruct((b, t, LANES), F32),
            jax.ShapeDtypeStruct((b, t, d), BF16),
            jax.ShapeDtypeStruct((b, t, d), BF16),
        ],
        compiler_params=_cparams(2),
        name="kv",
    )(x, modskv, modskv, wkv, wf, bf)


def _cumsum_kernel(lf_ref, o_ref, carry_scr):
    tc = lf_ref.shape[1]
    nh = o_ref.shape[1]

    @pl.when(pl.program_id(1) == 0)
    def _():
        carry_scr[...] = jnp.zeros(carry_scr.shape, F32)

    ft = lf_ref[0].T
    hi, mid, lo = _split3(ft)
    row = lax.broadcasted_iota(jnp.int32, (tc, tc), 0)
    col = lax.broadcasted_iota(jnp.int32, (tc, tc), 1)
    tri = jnp.where(row <= col, 1.0, 0.0).astype(BF16)
    c3 = _dot(jnp.concatenate([hi, mid, lo], axis=0), tri)
    cum = c3[0:LANES] + c3[LANES:2 * LANES] + c3[2 * LANES:3 * LANES] + carry_scr[:, 0:1]
    o_ref[0] = cum[0:nh]
    carry_scr[...] = jnp.broadcast_to(cum[:, tc - 1:tc], carry_scr.shape)


def _cumsum_heads(lf_pad, nh, *, tc):
    b, t, _ = lf_pad.shape
    return pl.pallas_call(
        _cumsum_kernel,
        grid=(b, t // tc),
        in_specs=[pl.BlockSpec((1, tc, LANES), lambda bi, ti: (bi, ti, 0))],
        out_specs=pl.BlockSpec((1, nh, tc), lambda bi, ti: (bi, 0, ti)),
        out_shape=jax.ShapeDtypeStruct((b, nh, t), F32),
        scratch_shapes=[pltpu.VMEM((LANES, LANES), F32)],
        compiler_params=_cparams(2),
        name="cumsum",
    )(lf_pad)


def _pair_queries(q2):
    lane = lax.broadcasted_iota(jnp.int32, q2.shape, 1)
    first = lane < (LANES // HEAD_PAIR)
    zero = jnp.zeros_like(q2)
    return jnp.concatenate([jnp.where(first, q2, zero), jnp.where(first, zero, q2)], axis=0)


def _pair_causal(tq, tk):
    row = lax.broadcasted_iota(jnp.int32, (2 * tq, tk), 0)
    col = lax.broadcasted_iota(jnp.int32, (2 * tq, tk), 1)
    return col <= jnp.where(row >= tq, row - tq, row)


def _softmax_step(carry, s, v2):
    m, l, acc = carry
    m_new = jnp.maximum(m, jnp.max(s, axis=-1, keepdims=True))
    a = jnp.exp(m - m_new)
    p = jnp.exp(s - m_new)
    l = a * l + jnp.sum(p, axis=-1, keepdims=True)
    acc = a * acc + _dot(p.astype(BF16), v2)
    return m_new, l, acc


def _pair_output(carry, tq):
    _, l, acc = carry
    o = acc / l
    lane = lax.broadcasted_iota(jnp.int32, (tq, LANES), 1)
    return jnp.where(lane < (LANES // HEAD_PAIR), o[0:tq], o[tq:2 * tq])


def _pair_bias(cum_ref, pair, kj, tq, tk):
    b0 = cum_ref[0, HEAD_PAIR * pair, pl.ds(kj, 1), :]
    b1 = cum_ref[0, HEAD_PAIR * pair + 1, pl.ds(kj, 1), :]
    return jnp.concatenate([jnp.broadcast_to(-b0, (tq, tk)), jnp.broadcast_to(-b1, (tq, tk))], axis=0)


def _attn_prompt_kernel(x_ref, sh_ref, sc_ref, gt_ref, wq_ref, wo_ref, k_ref, v_ref, cum_ref,
                        g_ref, b_ref, o_ref, q_scr, att_scr, *, alpha, scale):
    _, tq, d = x_ref.shape
    tk = tq
    qi = pl.program_id(1)
    x = x_ref[0]
    h = (x * (1.0 + sc_ref[0]) + sh_ref[0]).astype(BF16)
    q_scr[...] = (_dot(h, wq_ref[...]) * scale).astype(BF16)

    causal = _pair_causal(tq, tk)

    for pair in range(d // LANES):
        lanes = pl.ds(pair * LANES, LANES)
        qq = _pair_queries(q_scr[:, lanes])

        def k_step(kj, carry, pair=pair, lanes=lanes, qq=qq):
            r0 = pl.multiple_of(kj * tk, tk)
            s = _dot_nt(qq, k_ref[0, pl.ds(r0, tk), lanes]) + _pair_bias(cum_ref, pair, kj, tq, tk)
            return _softmax_step(carry, s, v_ref[0, pl.ds(r0, tk), lanes])

        init = (jnp.full((2 * tq, 1), NEG_BIG, F32), jnp.zeros((2 * tq, 1), F32),
                jnp.zeros((2 * tq, LANES), F32))
        carry = lax.fori_loop(0, qi, k_step, init)
        r0 = pl.multiple_of(qi * tk, tk)
        s = _dot_nt(qq, k_ref[0, pl.ds(r0, tk), lanes]) + _pair_bias(cum_ref, pair, qi, tq, tk)
        s = jnp.where(causal, s, NEG_BIG)
        carry = _softmax_step(carry, s, v_ref[0, pl.ds(r0, tk), lanes])
        att_scr[:, lanes] = _pair_output(carry, tq).astype(BF16)

    mix = _dot(att_scr[...], wo_ref[...])
    y = alpha * x + (1.0 + gt_ref[0]) * mix
    o_ref[0] = _layer_norm(y, g_ref[...], b_ref[...])


def _attn_prompt(x, mods, chunk0, wq, wo, kb, vb, cum, ln_g, ln_b, lnidx, *, alpha, tq):
    b, t, d = x.shape
    nh = cum.shape[1]
    scale = float(d // nh) ** -0.5
    cum4 = cum.reshape(b, nh, t // tq, tq)
    kern = functools.partial(_attn_prompt_kernel, alpha=alpha, scale=scale)
    return pl.pallas_call(
        kern,
        grid=(b, t // tq),
        in_specs=[
            pl.BlockSpec((1, tq, d), lambda bi, qi: (bi, qi, 0)),
            _mod_spec(1, d, chunk0 + 0),
            _mod_spec(1, d, chunk0 + 1),
            _mod_spec(1, d, chunk0 + 2),
            _resident((d, d), lambda bi, qi: (0, 0)),
            _resident((d, d), lambda bi, qi: (0, 0)),
            pl.BlockSpec((1, t, d), lambda bi, qi: (bi, 0, 0)),
            pl.BlockSpec((1, t, d), lambda bi, qi: (bi, 0, 0)),
            pl.BlockSpec((1, nh, t // tq, tq), lambda bi, qi: (bi, 0, 0, 0)),
            _resident((None, 1, d), lambda bi, qi: (lnidx, 0, 0)),
            _resident((None, 1, d), lambda bi, qi: (lnidx, 0, 0)),
        ],
        out_specs=pl.BlockSpec((1, tq, d), lambda bi, qi: (bi, qi, 0)),
        out_shape=jax.ShapeDtypeStruct((b, t, d), F32),
        scratch_shapes=[pltpu.VMEM((tq, d), BF16), pltpu.VMEM((tq, d), BF16)],
        compiler_params=_cparams(2),
        name="attn_prompt",
    )(x, mods, mods, mods, wq, wo, kb, vb, cum4, ln_g, ln_b)


def _attn_sample_kernel(x_ref, sh_ref, sc_ref, gt_ref, wq_ref, wo_ref, ck_ref, cv_ref, nk_ref,
                        nv_ref, cum_ref, g_ref, b_ref, o_ref, q_scr, att_scr,
                        *, alpha, scale, tk, n_past):
    _, tq, d = x_ref.shape
    tn = nk_ref.shape[1]
    x = x_ref[0]
    h = (x * (1.0 + sc_ref[0]) + sh_ref[0]).astype(BF16)
    q_scr[...] = (_dot(h, wq_ref[...]) * scale).astype(BF16)

    causal = _pair_causal(tq, tn)

    for pair in range(d // LANES):
        lanes = pl.ds(pair * LANES, LANES)
        qq = _pair_queries(q_scr[:, lanes])
        carry = (jnp.full((2 * tq, 1), NEG_BIG, F32), jnp.zeros((2 * tq, 1), F32),
                 jnp.zeros((2 * tq, LANES), F32))
        for kj in range(n_past):
            rows = pl.ds(kj * tk, tk)
            s = _dot_nt(qq, ck_ref[0, rows, lanes].astype(BF16)) + _pair_bias(cum_ref, pair, kj, tq, tk)
            carry = _softmax_step(carry, s, cv_ref[0, rows, lanes].astype(BF16))
        b0 = cum_ref[0, HEAD_PAIR * pair, n_past:n_past + 1, 0:tn]
        b1 = cum_ref[0, HEAD_PAIR * pair + 1, n_past:n_past + 1, 0:tn]
        bias = jnp.concatenate([jnp.broadcast_to(-b0, (tq, tn)), jnp.broadcast_to(-b1, (tq, tn))], axis=0)
        s = jnp.where(causal, _dot_nt(qq, nk_ref[0, :, lanes]) + bias, NEG_BIG)
        carry = _softmax_step(carry, s, nv_ref[0, :, lanes])
        att_scr[:, lanes] = _pair_output(carry, tq).astype(BF16)

    mix = _dot(att_scr[...], wo_ref[...])
    y = alpha * x + (1.0 + gt_ref[0]) * mix
    o_ref[0] = _layer_norm(y, g_ref[...], b_ref[...])


def _attn_sample(x, mods, chunk0, wq, wo, ck, cv, nk, nv, cum, ln_g, ln_b, lnidx,
                 *, alpha, tk):
    b, tq, d = x.shape
    past = ck.shape[1]
    tn = nk.shape[1]
    nh = cum.shape[1]
    n_past = past // tk
    scale = float(d // nh) ** -0.5
    cum4 = cum.reshape(b, nh, cum.shape[2] // tk, tk)
    kern = functools.partial(_attn_sample_kernel, alpha=alpha, scale=scale, tk=tk, n_past=n_past)
    per_b = lambda shape: pl.BlockSpec(shape, lambda bi: (bi,) + (0,) * (len(shape) - 1))
    mod = lambda c: pl.BlockSpec((None, 1, 1, d), lambda bi: (c, bi, 0, 0))
    return pl.pallas_call(
        kern,
        grid=(b,),
        in_specs=[
            per_b((1, tq, d)),
            mod(chunk0 + 0), mod(chunk0 + 1), mod(chunk0 + 2),
            _resident((d, d), lambda bi: (0, 0)),
            _resident((d, d), lambda bi: (0, 0)),
            per_b((1, past, d)), per_b((1, past, d)),
            per_b((1, tn, d)), per_b((1, tn, d)),
            per_b((1, nh, cum4.shape[2], tk)),
            _resident((None, 1, d), lambda bi: (lnidx, 0, 0)),
            _resident((None, 1, d), lambda bi: (lnidx, 0, 0)),
        ],
        out_specs=per_b((1, tq, d)),
        out_shape=jax.ShapeDtypeStruct((b, tq, d), F32),
        scratch_shapes=[pltpu.VMEM((tq, d), BF16), pltpu.VMEM((tq, d), BF16)],
        compiler_params=_cparams(1),
        name="attn_sample",
    )(x, mods, mods, mods, wq, wo, ck, cv, nk, nv, cum4, ln_g, ln_b)


def _tiles(t):
    return (512 if t % 512 == 0 else t), (256 if t % 256 == 0 else t)


def _trunk(x, conv_prev, kv_cache, mods, modskv, W, *, batched):
    b, t, d = x.shape
    depth = W["n_layers"]
    n_a = W["n_a"]
    nh = W["n_heads"]
    alpha = float((2 * depth) ** 0.25)
    tt, tq = _tiles(t)
    bb = b if batched else 1
    tt = t if batched else tt
    new_conv = []
    k_new = v_new = lf_new = None
    for l in range(depth):
        ch = l * 9
        if l == n_a:
            k_new, v_new, lf_new, lf_pad, kb, vb = _kv(
                x, modskv, W["wkv"], W["wf"], W["bf"], nh, bb=bb, tt=tt)
        x = _ffn(x, mods, ch + 0, W["wg"], W["wu"], W["wd"], (l, 0), W["ln_g"], W["ln_b"],
                 l * 3 + 0, alpha=alpha, bb=bb, tt=tt)
        if l < n_a:
            x, st = _conv(x, mods, ch + 3, conv_prev[l], W["pw1_w"][l], W["pw1_b"][l],
                          W["dw_w"][l], W["dw_b"][l], W["cln_g"][l], W["cln_b"][l],
                          W["pw2_w"][l], W["pw2_b"][l], W["ln_g"], W["ln_b"], l * 3 + 1,
                          alpha=alpha, bb=bb, tt=tt)
            new_conv.append(st)
        else:
            j = l - n_a
            if kv_cache is None:
                cum = _cumsum_heads(lf_pad, nh, tc=tt)
                x = _attn_prompt(x, mods, ch + 3, W["wq"][j], W["wo"][j], kb, vb, cum,
                                 W["ln_g"], W["ln_b"], l * 3 + 1, alpha=alpha, tq=tq)
            else:
                ck, cv, clf = kv_cache
                tk = 512
                past = ck.shape[1]
                assert past % tk == 0 and t <= LANES
                ck = ck.reshape(b, past, d)
                cv = cv.reshape(b, past, d)
                lf_all = jnp.concatenate(
                    [jnp.pad(clf, ((0, 0), (0, 0), (0, LANES - nh))), lf_pad], axis=1)
                lf_all = jnp.pad(lf_all, ((0, 0), (0, past + tk - lf_all.shape[1]), (0, 0)))
                cum = _cumsum_heads(lf_all, nh, tc=tk)
                nk = jnp.pad(kb, ((0, 0), (0, LANES - t), (0, 0)))
                nv = jnp.pad(vb, ((0, 0), (0, LANES - t), (0, 0)))
                x = _attn_sample(x, mods, ch + 3, W["wq"][j], W["wo"][j], ck, cv, nk, nv,
                                 cum, W["ln_g"], W["ln_b"], l * 3 + 1, alpha=alpha, tk=tk)
        x = _ffn(x, mods, ch + 6, W["wg"], W["wu"], W["wd"], (l, 1), W["ln_g"], W["ln_b"],
                 l * 3 + 2, alpha=alpha, bb=bb, tt=tt)
    return x, jnp.stack(new_conv), k_new, v_new, lf_new


def kernel(x_prompt, x_sample, cache_conv, cache_k, cache_v, cache_logf, c_prompt, c_sample,
           w_ada, b_ada, ln_g, ln_b, ffn_w_gate, ffn_w_up, ffn_w_down,
           pw1_w, pw1_b, dw_w, dw_b, conv_ln_g, conv_ln_b, pw2_w, pw2_b,
           w_ada_kv, b_ada_kv, w_k, w_v, w_f, b_f, w_q, w_o):
    bp, tp, d = x_prompt.shape
    bs, ts, _ = x_sample.shape
    depth = w_ada.shape[0]
    n_a = pw1_w.shape[0]
    nh = w_f.shape[1]
    assert d % LANES == 0 and d // nh == LANES // HEAD_PAIR and nh <= LANES
    assert ts % 16 == 0

    c_all = jnp.concatenate([c_prompt, c_sample], axis=0)
    mods = _ada(c_all, w_ada, b_ada)
    modskv = _ada(c_all, w_ada_kv[None], b_ada_kv[None])

    W = dict(
        n_layers=depth, n_a=n_a, n_heads=nh,
        wg=ffn_w_gate.astype(BF16), wu=ffn_w_up.astype(BF16), wd=ffn_w_down.astype(BF16),
        ln_g=ln_g.reshape(depth * 3, 1, d), ln_b=ln_b.reshape(depth * 3, 1, d),
        pw1_w=pw1_w.astype(BF16), pw1_b=pw1_b.reshape(n_a, 1, 2 * d),
        dw_w=dw_w, dw_b=dw_b.reshape(n_a, 1, d),
        cln_g=conv_ln_g.reshape(n_a, 1, d), cln_b=conv_ln_b.reshape(n_a, 1, d),
        pw2_w=pw2_w.astype(BF16), pw2_b=pw2_b.reshape(n_a, 1, d),
        wkv=jnp.concatenate([w_k, w_v], axis=1).astype(BF16),
        wf=jnp.pad(w_f, ((0, 0), (0, LANES - nh))).astype(BF16),
        bf=jnp.pad(b_f, (0, LANES - nh)).reshape(1, LANES),
        wq=w_q.astype(BF16), wo=w_o.astype(BF16),
    )

    zero_conv = jnp.zeros((n_a, bp, dw_w.shape[1] - 1, d), x_prompt.dtype)
    y_p, conv_p, k_p, v_p, lf_p = _trunk(x_prompt, zero_conv, None, mods[:, :bp], modskv[:, :bp], W, batched=False)
    y_s, conv_s, k_s, v_s, lf_s = _trunk(x_sample, cache_conv, (cache_k, cache_v, cache_logf),
                                         mods[:, bp:], modskv[:, bp:], W, batched=True)
    hd = d // nh
    return (y_p, y_s, conv_p,
            k_p.reshape(bp, tp, nh, hd), v_p.reshape(bp, tp, nh, hd), lf_p,
            conv_s,
            k_s.reshape(bs, ts, nh, hd), v_s.reshape(bs, ts, nh, hd), lf_s)
```

```python
import functools

import jax
import jax.numpy as jnp
from jax import lax
from jax.experimental import pallas as pl
from jax.experimental.pallas import tpu as pltpu

F32 = jnp.float32
BF16 = jnp.bfloat16

LN_EPS = 1e-5
LANES = 128
SUBLANES = 8
HEAD_PAIR = 2
CONV_HALO = 32
VMEM_LIMIT_BYTES = 56 * 1024 * 1024
CONV_TBLOCK = 32
ATTN_GROUP = 4
NEG_BIG = -1e30
LOG2E = 1.4426950408889634


def _cparams(n_grid):
    return pltpu.CompilerParams(
        dimension_semantics=("arbitrary",) * n_grid,
        vmem_limit_bytes=VMEM_LIMIT_BYTES,
    )


def _resident(shape, index_map):
    return pl.BlockSpec(shape, index_map, pipeline_mode=pl.Buffered(1))


def _layer_norm(y, g, b):
    mu = jnp.mean(y, axis=-1, keepdims=True)
    yc = y - mu
    var = jnp.mean(yc * yc, axis=-1, keepdims=True)
    return yc * lax.rsqrt(var + LN_EPS) * g + b


def _dot(a, b):
    return jnp.dot(a, b, preferred_element_type=F32)


def _dot_nt(a, b):
    return lax.dot_general(a, b, (((1,), (1,)), ((), ())), preferred_element_type=F32)


def _split3(x):
    hi = x.astype(BF16)
    r1 = x - hi.astype(F32)
    mid = r1.astype(BF16)
    lo = (r1 - mid.astype(F32)).astype(BF16)
    return hi, mid, lo


def _ada_kernel(c_ref, w_ref, b_ref, o_ref):
    c = c_ref[...]
    ca = (c * jax.nn.sigmoid(c)).astype(BF16)
    o_ref[...] = _dot(ca, w_ref[...].astype(BF16)) + b_ref[...]


def _ada(c, w, b):
    bc, d = c.shape
    nl, _, n = w.shape
    nch = n // d
    out = pl.pallas_call(
        _ada_kernel,
        grid=(nl, nch),
        in_specs=[
            _resident((bc, d), lambda l, j: (0, 0)),
            pl.BlockSpec((None, d, d), lambda l, j: (l, 0, j)),
            pl.BlockSpec((None, 1, d), lambda l, j: (l * nch + j, 0, 0)),
        ],
        out_specs=pl.BlockSpec((None, bc, d), lambda l, j: (l * nch + j, 0, 0)),
        out_shape=jax.ShapeDtypeStruct((nl * nch, bc, d), F32),
        compiler_params=_cparams(2),
        name="ada",
    )(c, w, b.reshape(nl * nch, 1, d))
    return out.reshape(nl * nch, bc, 1, d)


def _mod_spec(bb, d, chunk):
    return pl.BlockSpec((None, bb, 1, d), lambda b, t: (chunk, b, 0, 0))


def _ffn_kernel(x_ref, sh_ref, sc_ref, gt_ref, wg_ref, wu_ref, wd_ref, g_ref, b_ref, o_ref,
                *, alpha, ff_chunk):
    bb, tt, d = x_ref.shape
    dff = wg_ref.shape[1]
    x = x_ref[...]
    h = (x * (1.0 + sc_ref[...]) + sh_ref[...]).reshape(bb * tt, d).astype(BF16)
    acc = None
    for c0 in range(0, dff, ff_chunk):
        c1 = min(c0 + ff_chunk, dff)
        g = _dot(h, wg_ref[:, c0:c1])
        u = _dot(h, wu_ref[:, c0:c1])
        a = (g * jax.nn.sigmoid(g) * u).astype(BF16)
        part = _dot(a, wd_ref[c0:c1, :])
        acc = part if acc is None else acc + part
    f = acc.reshape(bb, tt, d)
    y = alpha * x + (0.5 * (1.0 + gt_ref[...])) * f
    o_ref[...] = _layer_norm(y, g_ref[...], b_ref[...])


def _ffn(x, mods, chunk0, wg, wu, wd, widx, ln_g, ln_b, lnidx, *, alpha, bb, tt):
    b, t, d = x.shape
    dff = wg.shape[-1]
    l, i = widx
    kern = functools.partial(_ffn_kernel, alpha=alpha, ff_chunk=512)
    return pl.pallas_call(
        kern,
        grid=(b // bb, t // tt),
        in_specs=[
            pl.BlockSpec((bb, tt, d), lambda bi, ti: (bi, ti, 0)),
            _mod_spec(bb, d, chunk0 + 0),
            _mod_spec(bb, d, chunk0 + 1),
            _mod_spec(bb, d, chunk0 + 2),
            _resident((None, None, d, dff), lambda bi, ti: (l, i, 0, 0)),
            _resident((None, None, d, dff), lambda bi, ti: (l, i, 0, 0)),
            _resident((None, None, dff, d), lambda bi, ti: (l, i, 0, 0)),
            _resident((None, 1, d), lambda bi, ti: (lnidx, 0, 0)),
            _resident((None, 1, d), lambda bi, ti: (lnidx, 0, 0)),
        ],
        out_specs=pl.BlockSpec((bb, tt, d), lambda bi, ti: (bi, ti, 0)),
        out_shape=jax.ShapeDtypeStruct((b, t, d), F32),
        compiler_params=_cparams(2),
        name="ffn",
    )(x, mods, mods, mods, wg, wu, wd, ln_g, ln_b)


def _conv_kernel(x_ref, sh_ref, sc_ref, gt_ref, prev_ref, w1_ref, b1_ref, dw_ref, dwb_ref,
                 cg_ref, cb_ref, w2_ref, b2_ref, g_ref, b_ref, o_ref, st_ref, u_scr, ut_scr, zt_scr,
                 *, alpha, conv_k, tblock):
    bb, tt, d = x_ref.shape
    ti = pl.program_id(1)
    keep = conv_k - 1
    off = CONV_HALO - keep
    nl = d // LANES

    @pl.when(ti == 0)
    def _():
        u_scr[:, 0:off, :] = jnp.zeros((bb, off, d), F32)
        u_scr[:, off:CONV_HALO, :] = prev_ref[...]

    x = x_ref[...]
    h = (x * (1.0 + sc_ref[...]) + sh_ref[...]).reshape(bb * tt, d).astype(BF16)
    u2 = _dot(h, w1_ref[...]) + b1_ref[...]
    u = u2[:, :d] * jax.nn.sigmoid(u2[:, d:])
    u_scr[:, CONV_HALO:CONV_HALO + tt, :] = u.reshape(bb, tt, d)

    for bi in range(bb):
        for j in range(nl):
            ut_scr[bi, pl.ds(j, CONV_HALO + tt, stride=nl), :] = u_scr[bi, :, j * LANES:(j + 1) * LANES]
    n_blocks = tt // tblock

    def block_body(it, carry):
        bi = it // n_blocks
        base = pl.multiple_of((it % n_blocks) * (tblock * nl), tblock * nl)
        accs = [jnp.zeros((nl, LANES), F32) for _ in range(tblock)]
        for k in range(conv_k):
            w = dw_ref[k]
            for i in range(tblock):
                accs[i] = accs[i] + w * ut_scr[bi, pl.ds(base + (i + off + k) * nl, nl), :]
        for i in range(tblock):
            zt_scr[bi, pl.ds(base + i * nl, nl), :] = accs[i]
        return carry

    lax.fori_loop(0, bb * n_blocks, block_body, 0)

    st_ref[...] = u_scr[:, tt + off:tt + CONV_HALO, :]
    if tt >= CONV_HALO:
        u_scr[:, 0:CONV_HALO, :] = u_scr[:, tt:tt + CONV_HALO, :]

    z = jnp.stack([
        jnp.concatenate([zt_scr[bi, pl.ds(j, tt, stride=nl), :] for j in range(nl)], axis=1)
        for bi in range(bb)])
    z = z + dwb_ref[...]
    z = _layer_norm(z, cg_ref[...], cb_ref[...])
    z = (z * jax.nn.sigmoid(z)).reshape(bb * tt, d).astype(BF16)
    mix = (_dot(z, w2_ref[...]) + b2_ref[...]).reshape(bb, tt, d)
    y = alpha * x + (1.0 + gt_ref[...]) * mix
    o_ref[...] = _layer_norm(y, g_ref[...], b_ref[...])


def _conv(x, mods, chunk0, prev, w1, b1, dw, dwb, cg, cb, w2, b2, ln_g, ln_b, lnidx,
          *, alpha, bb, tt):
    b, t, d = x.shape
    conv_k = dw.shape[0]
    keep = conv_k - 1
    assert keep <= CONV_HALO and (t == tt or tt >= CONV_HALO)
    nl = d // LANES
    assert nl == SUBLANES
    tblock = min(tt, CONV_TBLOCK)
    kern = functools.partial(_conv_kernel, alpha=alpha, conv_k=conv_k, tblock=tblock)
    vec = lambda: _resident((1, d), lambda bi, ti: (0, 0))
    return pl.pallas_call(
        kern,
        grid=(b // bb, t // tt),
        in_specs=[
            pl.BlockSpec((bb, tt, d), lambda bi, ti: (bi, ti, 0)),
            _mod_spec(bb, d, chunk0 + 0),
            _mod_spec(bb, d, chunk0 + 1),
            _mod_spec(bb, d, chunk0 + 2),
            pl.BlockSpec((bb, keep, d), lambda bi, ti: (bi, 0, 0)),
            _resident((d, 2 * d), lambda bi, ti: (0, 0)),
            _resident((1, 2 * d), lambda bi, ti: (0, 0)),
            _resident((conv_k, nl, LANES), lambda bi, ti: (0, 0, 0)),
            vec(), vec(), vec(),
            _resident((d, d), lambda bi, ti: (0, 0)),
            vec(),
            _resident((None, 1, d), lambda bi, ti: (lnidx, 0, 0)),
            _resident((None, 1, d), lambda bi, ti: (lnidx, 0, 0)),
        ],
        out_specs=[
            pl.BlockSpec((bb, tt, d), lambda bi, ti: (bi, ti, 0)),
            pl.BlockSpec((bb, keep, d), lambda bi, ti: (bi, 0, 0)),
        ],
        out_shape=[
            jax.ShapeDtypeStruct((b, t, d), F32),
            jax.ShapeDtypeStruct((b, keep, d), F32),
        ],
        scratch_shapes=[
            pltpu.VMEM((bb, CONV_HALO + tt, d), F32),
            pltpu.VMEM((bb, (CONV_HALO + tt) * nl, LANES), F32),
            pltpu.VMEM((bb, tt * nl, LANES), F32),
        ],
        compiler_params=_cparams(2),
        name="conv",
    )(x, mods, mods, mods, prev, w1, b1, dw.reshape(conv_k, nl, LANES), dwb, cg, cb, w2, b2, ln_g, ln_b)


def _log_sigmoid(f):
    return jnp.minimum(f, 0.0) - jnp.log1p(jnp.exp(-jnp.abs(f)))


def _kv_kernel(x_ref, sh_ref, sc_ref, wkv_ref, wf_ref, bf_ref,
               k_ref, v_ref, lf_ref, lfp_ref, kb_ref, vb_ref):
    bb, tt, d = x_ref.shape
    nh = lf_ref.shape[-1]
    x = x_ref[...]
    h = (x * (1.0 + sc_ref[...]) + sh_ref[...]).reshape(bb * tt, d).astype(BF16)
    kv = _dot(h, wkv_ref[...])
    k = kv[:, :d]
    v = kv[:, d:]
    kb_ref[...] = k.astype(BF16).reshape(bb, tt, d)
    vb_ref[...] = v.astype(BF16).reshape(bb, tt, d)
    hd = d // nh
    for bi in range(bb):
        for hh in range(nh):
            rows = pl.ds(hh, tt, stride=nh)
            k_ref[bi, rows, :] = k[bi * tt:(bi + 1) * tt, hh * hd:(hh + 1) * hd]
            v_ref[bi, rows, :] = v[bi * tt:(bi + 1) * tt, hh * hd:(hh + 1) * hd]
    lf = _log_sigmoid(_dot(h, wf_ref[...]) + bf_ref[...])
    lfp_ref[...] = lf.reshape(bb, tt, LANES)
    lf_ref[...] = lf[:, :nh].reshape(bb, tt, nh)


def _kv(x, modskv, wkv, wf, bf, nh, *, bb, tt):
    b, t, d = x.shape
    blk = lambda w: pl.BlockSpec((bb, tt, w), lambda bi, ti: (bi, ti, 0))
    head_major = pl.BlockSpec((bb, tt * nh, d // nh), lambda bi, ti: (bi, ti, 0))
    return pl.pallas_call(
        _kv_kernel,
        grid=(b // bb, t // tt),
        in_specs=[
            blk(d),
            _mod_spec(bb, d, 0),
            _mod_spec(bb, d, 1),
            _resident((d, 2 * d), lambda bi, ti: (0, 0)),
            _resident((d, LANES), lambda bi, ti: (0, 0)),
            _resident((1, LANES), lambda bi, ti: (0, 0)),
        ],
        out_specs=[head_major, head_major, blk(nh), blk(LANES), blk(d), blk(d)],
        out_shape=[
            jax.ShapeDtypeStruct((b, t * nh, d // nh), F32),
            jax.ShapeDtypeStruct((b, t * nh, d // nh), F32),
            jax.ShapeDtypeStruct((b, t, nh), F32),
            jax.ShapeDtypeStruct((b, t, LANES), F32),
            jax.ShapeDtypeStruct((b, t, d), BF16),
            jax.ShapeDtypeStruct((b, t, d), BF16),
        ],
        compiler_params=_cparams(2),
        name="kv",
    )(x, modskv, modskv, wkv, wf, bf)


def _cumsum_kernel(lf_ref, o_ref, *, tc):
    t = lf_ref.shape[1]
    nh = o_ref.shape[1]
    row = lax.broadcasted_iota(jnp.int32, (tc, tc), 0)
    col = lax.broadcasted_iota(jnp.int32, (tc, tc), 1)
    tri = jnp.where(row <= col, 1.0, 0.0).astype(BF16)
    carry = jnp.zeros((LANES, 1), F32)
    for c0 in range(0, t, tc):
        ft = lf_ref[0, c0:c0 + tc, :].T
        hi, mid, lo = _split3(ft)
        c3 = _dot(jnp.concatenate([hi, mid, lo], axis=0), tri)
        cum = c3[0:LANES] + c3[LANES:2 * LANES] + c3[2 * LANES:3 * LANES] + carry
        o_ref[0, :, c0:c0 + tc] = cum[0:nh]
        carry = cum[:, tc - 1:tc]


def _cumsum_heads(lf_pad, nh, *, tc):
    b, t, _ = lf_pad.shape
    assert t % tc == 0
    return pl.pallas_call(
        functools.partial(_cumsum_kernel, tc=tc),
        grid=(b,),
        in_specs=[pl.BlockSpec((1, t, LANES), lambda bi: (bi, 0, 0))],
        out_specs=pl.BlockSpec((1, nh, t), lambda bi: (bi, 0, 0)),
        out_shape=jax.ShapeDtypeStruct((b, nh, t), F32),
        compiler_params=_cparams(1),
        name="cumsum",
    )(lf_pad)


def _pair_queries(q2):
    lane = lax.broadcasted_iota(jnp.int32, q2.shape, 1)
    first = lane < (LANES // HEAD_PAIR)
    zero = jnp.zeros_like(q2)
    return jnp.concatenate([jnp.where(first, q2, zero), jnp.where(first, zero, q2)], axis=0)


def _pair_causal(tq, tk):
    row = lax.broadcasted_iota(jnp.int32, (2 * tq, tk), 0)
    col = lax.broadcasted_iota(jnp.int32, (2 * tq, tk), 1)
    return col <= jnp.where(row >= tq, row - tq, row)


def _pair_bias(cum_ref, pair, kj, tq, tk, mult):
    b0 = cum_ref[0, HEAD_PAIR * pair, pl.ds(kj, 1), :] * (-mult)
    b1 = cum_ref[0, HEAD_PAIR * pair + 1, pl.ds(kj, 1), :] * (-mult)
    return jnp.concatenate([jnp.broadcast_to(b0, (tq, tk)), jnp.broadcast_to(b1, (tq, tk))], axis=0)


def _attn_prompt_kernel(x_ref, sh_ref, sc_ref, gt_ref, wq_ref, wo_ref, k_ref, v_ref, cum_ref,
                        g_ref, b_ref, o_ref, qq_scr, m_scr, acc_scr, att_scr,
                        *, alpha, scale):
    _, tq, d = x_ref.shape
    tk = tq
    n_pair = d // LANES
    qi = pl.program_id(1)
    x = x_ref[0]
    h = (x * (1.0 + sc_ref[0]) + sh_ref[0]).astype(BF16)
    q = (_dot(h, wq_ref[...]) * (scale * LOG2E)).astype(BF16)
    for pair in range(n_pair):
        qq_scr[pair] = _pair_queries(q[:, pair * LANES:(pair + 1) * LANES])
    n_grp = m_scr.shape[0]
    ones = jnp.ones((tk, LANES), BF16)
    lane = lax.broadcasted_iota(jnp.int32, (tq, LANES), 1)

    def key_tile(kj, pair0, masked):
        r0 = pl.multiple_of(kj * tk, tk)
        for g in range(n_grp):
            pair = pair0 + g
            lanes = pl.ds(pair * LANES, LANES)
            s = _dot_nt(qq_scr[pair], k_ref[0, pl.ds(r0, tk), lanes])
            s = s + _pair_bias(cum_ref, pair, kj, tq, tk, LOG2E)
            if masked:
                s = jnp.where(_pair_causal(tq, tk), s, NEG_BIG)
            m_old = m_scr[g]
            m_new = jnp.maximum(m_old, jnp.broadcast_to(jnp.max(s, axis=-1, keepdims=True), m_old.shape))
            a = jnp.exp2(m_old - m_new)
            p = jnp.exp2(s - jnp.concatenate([m_new] * (tk // LANES), axis=1))
            v_ext = jnp.concatenate([v_ref[0, pl.ds(r0, tk), lanes], ones], axis=1)
            acc_scr[g] = jnp.concatenate([a, a], axis=1) * acc_scr[g] + _dot(p.astype(BF16), v_ext)
            m_scr[g] = m_new

    for pair0 in range(0, n_pair, n_grp):
        m_scr[...] = jnp.full(m_scr.shape, NEG_BIG, F32)
        acc_scr[...] = jnp.zeros(acc_scr.shape, F32)

        def full_tile(kj, carry, pair0=pair0):
            key_tile(kj, pair0, False)
            return carry

        lax.fori_loop(0, qi, full_tile, 0)
        key_tile(qi, pair0, True)
        for g in range(n_grp):
            o = acc_scr[g, :, 0:LANES] / acc_scr[g, :, LANES:2 * LANES]
            o = jnp.where(lane < (LANES // HEAD_PAIR), o[0:tq], o[tq:2 * tq])
            att_scr[:, (pair0 + g) * LANES:(pair0 + g + 1) * LANES] = o.astype(BF16)

    mix = _dot(att_scr[...], wo_ref[...])
    y = alpha * x + (1.0 + gt_ref[0]) * mix
    o_ref[0] = _layer_norm(y, g_ref[...], b_ref[...])


def _attn_prompt(x, mods, chunk0, wq, wo, kb, vb, cum, ln_g, ln_b, lnidx, *, alpha, tq):
    b, t, d = x.shape
    nh = cum.shape[1]
    scale = float(d // nh) ** -0.5
    cum4 = cum.reshape(b, nh, t // tq, tq)
    grp = min(ATTN_GROUP, d // LANES)
    assert (d // LANES) % grp == 0
    kern = functools.partial(_attn_prompt_kernel, alpha=alpha, scale=scale)
    return pl.pallas_call(
        kern,
        grid=(b, t // tq),
        in_specs=[
            pl.BlockSpec((1, tq, d), lambda bi, qi: (bi, qi, 0)),
            _mod_spec(1, d, chunk0 + 0),
            _mod_spec(1, d, chunk0 + 1),
            _mod_spec(1, d, chunk0 + 2),
            _resident((d, d), lambda bi, qi: (0, 0)),
            _resident((d, d), lambda bi, qi: (0, 0)),
            pl.BlockSpec((1, t, d), lambda bi, qi: (bi, 0, 0)),
            pl.BlockSpec((1, t, d), lambda bi, qi: (bi, 0, 0)),
            pl.BlockSpec((1, nh, t // tq, tq), lambda bi, qi: (bi, 0, 0, 0)),
            _resident((None, 1, d), lambda bi, qi: (lnidx, 0, 0)),
            _resident((None, 1, d), lambda bi, qi: (lnidx, 0, 0)),
        ],
        out_specs=pl.BlockSpec((1, tq, d), lambda bi, qi: (bi, qi, 0)),
        out_shape=jax.ShapeDtypeStruct((b, t, d), F32),
        scratch_shapes=[
            pltpu.VMEM((d // LANES, 2 * tq, LANES), BF16),
            pltpu.VMEM((grp, 2 * tq, LANES), F32),
            pltpu.VMEM((grp, 2 * tq, 2 * LANES), F32),
            pltpu.VMEM((tq, d), BF16),
        ],
        compiler_params=_cparams(2),
        name="attn_prompt",
    )(x, mods, mods, mods, wq, wo, kb, vb, cum4, ln_g, ln_b)


def _attn_sample_kernel(x_ref, sh_ref, sc_ref, gt_ref, wq_ref, wo_ref, ck_ref, cv_ref, nk_ref,
                        nv_ref, cum_ref, g_ref, b_ref, o_ref, att_scr, *, alpha, scale):
    _, tq, d = x_ref.shape
    past = ck_ref.shape[1]
    tn = nk_ref.shape[1]
    x = x_ref[0]
    h = (x * (1.0 + sc_ref[0]) + sh_ref[0]).astype(BF16)
    q = (_dot(h, wq_ref[...]) * scale).astype(BF16)
    causal = _pair_causal(tq, tn)
    lane = lax.broadcasted_iota(jnp.int32, (tq, LANES), 1)

    def pair_bias(pair, lo, n):
        b0 = cum_ref[0, HEAD_PAIR * pair:HEAD_PAIR * pair + 1, lo:lo + n]
        b1 = cum_ref[0, HEAD_PAIR * pair + 1:HEAD_PAIR * pair + 2, lo:lo + n]
        return jnp.concatenate([jnp.broadcast_to(-b0, (tq, n)), jnp.broadcast_to(-b1, (tq, n))], axis=0)

    for pair in range(d // LANES):
        lanes = pl.ds(pair * LANES, LANES)
        qq = _pair_queries(q[:, pair * LANES:(pair + 1) * LANES])
        s_p = _dot_nt(qq, ck_ref[0, :, lanes].astype(BF16)) + pair_bias(pair, 0, past)
        s_n = _dot_nt(qq, nk_ref[0, :, lanes]) + pair_bias(pair, past, tn)
        s_n = jnp.where(causal, s_n, NEG_BIG)
        m = jnp.maximum(jnp.max(s_p, axis=-1, keepdims=True), jnp.max(s_n, axis=-1, keepdims=True))
        p_p = jnp.exp(s_p - m)
        p_n = jnp.exp(s_n - m)
        l = jnp.sum(p_p, axis=-1, keepdims=True) + jnp.sum(p_n, axis=-1, keepdims=True)
        o = _dot(p_p.astype(BF16), cv_ref[0, :, lanes].astype(BF16)) + _dot(p_n.astype(BF16), nv_ref[0, :, lanes])
        o = o / l
        att_scr[:, lanes] = jnp.where(lane < (LANES // HEAD_PAIR), o[0:tq], o[tq:2 * tq]).astype(BF16)

    mix = _dot(att_scr[...], wo_ref[...])
    y = alpha * x + (1.0 + gt_ref[0]) * mix
    o_ref[0] = _layer_norm(y, g_ref[...], b_ref[...])


def _attn_sample(x, mods, chunk0, wq, wo, ck, cv, nk, nv, cum, ln_g, ln_b, lnidx, *, alpha):
    b, tq, d = x.shape
    past = ck.shape[1]
    tn = nk.shape[1]
    nh = cum.shape[1]
    scale = float(d // nh) ** -0.5
    kern = functools.partial(_attn_sample_kernel, alpha=alpha, scale=scale)
    per_b = lambda shape: pl.BlockSpec(shape, lambda bi: (bi,) + (0,) * (len(shape) - 1))
    mod = lambda c: pl.BlockSpec((None, 1, 1, d), lambda bi: (c, bi, 0, 0))
    return pl.pallas_call(
        kern,
        grid=(b,),
        in_specs=[
            per_b((1, tq, d)),
            mod(chunk0 + 0), mod(chunk0 + 1), mod(chunk0 + 2),
            _resident((d, d), lambda bi: (0, 0)),
            _resident((d, d), lambda bi: (0, 0)),
            per_b((1, past, d)), per_b((1, past, d)),
            per_b((1, tn, d)), per_b((1, tn, d)),
            per_b((1, nh, cum.shape[2])),
            _resident((None, 1, d), lambda bi: (lnidx, 0, 0)),
            _resident((None, 1, d), lambda bi: (lnidx, 0, 0)),
        ],
        out_specs=per_b((1, tq, d)),
        out_shape=jax.ShapeDtypeStruct((b, tq, d), F32),
        scratch_shapes=[pltpu.VMEM((tq, d), BF16)],
        compiler_params=_cparams(1),
        name="attn_sample",
    )(x, mods, mods, mods, wq, wo, ck, cv, nk, nv, cum, ln_g, ln_b)


def _tiles(t):
    return (512 if t % 512 == 0 else t), (512 if t % 512 == 0 else t)


def _trunk(x, conv_prev, kv_cache, mods, modskv, W, *, batched):
    b, t, d = x.shape
    depth = W["n_layers"]
    n_a = W["n_a"]
    nh = W["n_heads"]
    alpha = float((2 * depth) ** 0.25)
    tt, tq = _tiles(t)
    bb = b if batched else 1
    tt = t if batched else tt
    new_conv = []
    k_new = v_new = lf_new = None
    for l in range(depth):
        ch = l * 9
        if l == n_a:
            k_new, v_new, lf_new, lf_pad, kb, vb = _kv(
                x, modskv, W["wkv"], W["wf"], W["bf"], nh, bb=bb, tt=tt)
        x = _ffn(x, mods, ch + 0, W["wg"], W["wu"], W["wd"], (l, 0), W["ln_g"], W["ln_b"],
                 l * 3 + 0, alpha=alpha, bb=bb, tt=tt)
        if l < n_a:
            x, st = _conv(x, mods, ch + 3, conv_prev[l], W["pw1_w"][l], W["pw1_b"][l],
                          W["dw_w"][l], W["dw_b"][l], W["cln_g"][l], W["cln_b"][l],
                          W["pw2_w"][l], W["pw2_b"][l], W["ln_g"], W["ln_b"], l * 3 + 1,
                          alpha=alpha, bb=bb, tt=tt)
            new_conv.append(st)
        else:
            j = l - n_a
            if kv_cache is None:
                cum = _cumsum_heads(lf_pad, nh, tc=tt)
                x = _attn_prompt(x, mods, ch + 3, W["wq"][j], W["wo"][j], kb, vb, cum,
                                 W["ln_g"], W["ln_b"], l * 3 + 1, alpha=alpha, tq=tq)
            else:
                ck, cv, clf = kv_cache
                tk = 512
                past = ck.shape[1]
                assert past % tk == 0 and t <= LANES
                ck = ck.reshape(b, past, d)
                cv = cv.reshape(b, past, d)
                lf_all = jnp.concatenate(
                    [jnp.pad(clf, ((0, 0), (0, 0), (0, LANES - nh))), lf_pad], axis=1)
                lf_all = jnp.pad(lf_all, ((0, 0), (0, past + tk - lf_all.shape[1]), (0, 0)))
                cum = _cumsum_heads(lf_all, nh, tc=tk)
                nk = jnp.pad(kb, ((0, 0), (0, LANES - t), (0, 0)))
                nv = jnp.pad(vb, ((0, 0), (0, LANES - t), (0, 0)))
                x = _attn_sample(x, mods, ch + 3, W["wq"][j], W["wo"][j], ck, cv, nk, nv,
                                 cum, W["ln_g"], W["ln_b"], l * 3 + 1, alpha=alpha)
        x = _ffn(x, mods, ch + 6, W["wg"], W["wu"], W["wd"], (l, 1), W["ln_g"], W["ln_b"],
                 l * 3 + 2, alpha=alpha, bb=bb, tt=tt)
    return x, jnp.stack(new_conv), k_new, v_new, lf_new


def kernel(x_prompt, x_sample, cache_conv, cache_k, cache_v, cache_logf, c_prompt, c_sample,
           w_ada, b_ada, ln_g, ln_b, ffn_w_gate, ffn_w_up, ffn_w_down,
           pw1_w, pw1_b, dw_w, dw_b, conv_ln_g, conv_ln_b, pw2_w, pw2_b,
           w_ada_kv, b_ada_kv, w_k, w_v, w_f, b_f, w_q, w_o):
    bp, tp, d = x_prompt.shape
    bs, ts, _ = x_sample.shape
    depth = w_ada.shape[0]
    n_a = pw1_w.shape[0]
    nh = w_f.shape[1]
    assert d % LANES == 0 and d // nh == LANES // HEAD_PAIR and nh <= LANES
    assert ts % 16 == 0

    c_all = jnp.concatenate([c_prompt, c_sample], axis=0)
    mods = _ada(c_all, w_ada, b_ada)
    modskv = _ada(c_all, w_ada_kv[None], b_ada_kv[None])

    W = dict(
        n_layers=depth, n_a=n_a, n_heads=nh,
        wg=ffn_w_gate.astype(BF16), wu=ffn_w_up.astype(BF16), wd=ffn_w_down.astype(BF16),
        ln_g=ln_g.reshape(depth * 3, 1, d), ln_b=ln_b.reshape(depth * 3, 1, d),
        pw1_w=pw1_w.astype(BF16), pw1_b=pw1_b.reshape(n_a, 1, 2 * d),
        dw_w=dw_w, dw_b=dw_b.reshape(n_a, 1, d),
        cln_g=conv_ln_g.reshape(n_a, 1, d), cln_b=conv_ln_b.reshape(n_a, 1, d),
        pw2_w=pw2_w.astype(BF16), pw2_b=pw2_b.reshape(n_a, 1, d),
        wkv=jnp.concatenate([w_k, w_v], axis=1).astype(BF16),
        wf=jnp.pad(w_f, ((0, 0), (0, LANES - nh))).astype(BF16),
        bf=jnp.pad(b_f, (0, LANES - nh)).reshape(1, LANES),
        wq=w_q.astype(BF16), wo=w_o.astype(BF16),
    )

    zero_conv = jnp.zeros((n_a, bp, dw_w.shape[1] - 1, d), x_prompt.dtype)
    y_p, conv_p, k_p, v_p, lf_p = _trunk(x_prompt, zero_conv, None, mods[:, :bp], modskv[:, :bp], W, batched=False)
    y_s, conv_s, k_s, v_s, lf_s = _trunk(x_sample, cache_conv, (cache_k, cache_v, cache_logf),
                                         mods[:, bp:], modskv[:, bp:], W, batched=True)
    hd = d // nh
    return (y_p, y_s, conv_p,
            k_p.reshape(bp, tp, nh, hd), v_p.reshape(bp, tp, nh, hd), lf_p,
            conv_s,
            k_s.reshape(bs, ts, nh, hd), v_s.reshape(bs, ts, nh, hd), lf_s)
```

```python
import functools

import jax
import jax.numpy as jnp
from jax import lax
from jax.experimental import pallas as pl
from jax.experimental.pallas import tpu as pltpu

F32 = jnp.float32
BF16 = jnp.bfloat16

LN_EPS = 1e-5
LANES = 128
SUBLANES = 8
HEAD_PAIR = 2
CONV_HALO = 32
VMEM_LIMIT_BYTES = 56 * 1024 * 1024
CONV_TBLOCK = 32
ATTN_GROUP = 4
NEG_BIG = -1e30
LOG2E = 1.4426950408889634


def _cparams(n_grid):
    return pltpu.CompilerParams(
        dimension_semantics=("arbitrary",) * n_grid,
        vmem_limit_bytes=VMEM_LIMIT_BYTES,
    )


def _resident(shape, index_map):
    return pl.BlockSpec(shape, index_map, pipeline_mode=pl.Buffered(1))


def _layer_norm(y, g, b):
    mu = jnp.mean(y, axis=-1, keepdims=True)
    yc = y - mu
    var = jnp.mean(yc * yc, axis=-1, keepdims=True)
    return yc * lax.rsqrt(var + LN_EPS) * g + b


def _dot(a, b):
    return jnp.dot(a, b, preferred_element_type=F32)


def _dot_nt(a, b):
    return lax.dot_general(a, b, (((1,), (1,)), ((), ())), preferred_element_type=F32)


def _split3(x):
    hi = x.astype(BF16)
    r1 = x - hi.astype(F32)
    mid = r1.astype(BF16)
    lo = (r1 - mid.astype(F32)).astype(BF16)
    return hi, mid, lo


def _ada_kernel(c_ref, w_ref, b_ref, o_ref):
    c = c_ref[...]
    ca = (c * jax.nn.sigmoid(c)).astype(BF16)
    o_ref[...] = _dot(ca, w_ref[...].astype(BF16)) + b_ref[...]


def _ada(c, w, b):
    bc, d = c.shape
    nl, _, n = w.shape
    nch = n // d
    out = pl.pallas_call(
        _ada_kernel,
        grid=(nl, nch),
        in_specs=[
            _resident((bc, d), lambda l, j: (0, 0)),
            pl.BlockSpec((None, d, d), lambda l, j: (l, 0, j)),
            pl.BlockSpec((None, 1, d), lambda l, j: (l * nch + j, 0, 0)),
        ],
        out_specs=pl.BlockSpec((None, bc, d), lambda l, j: (l * nch + j, 0, 0)),
        out_shape=jax.ShapeDtypeStruct((nl * nch, bc, d), F32),
        compiler_params=_cparams(2),
        name="ada",
    )(c, w, b.reshape(nl * nch, 1, d))
    return out.reshape(nl * nch, bc, 1, d)


def _mod_spec(bb, d, chunk):
    return pl.BlockSpec((None, bb, 1, d), lambda b, t: (chunk, b, 0, 0))


def _ffn_kernel(x_ref, sh_ref, sc_ref, gt_ref, wg_ref, wu_ref, wd_ref, g_ref, b_ref, o_ref,
                *, alpha, ff_chunk, sub_t):
    bb, tt, d = x_ref.shape
    dff = wg_ref.shape[1]
    for t0 in range(0, tt, sub_t):
        x = x_ref[:, t0:t0 + sub_t, :]
        h = (x * (1.0 + sc_ref[...]) + sh_ref[...]).reshape(bb * sub_t, d).astype(BF16)
        acc = None
        for c0 in range(0, dff, ff_chunk):
            c1 = min(c0 + ff_chunk, dff)
            g = _dot(h, wg_ref[:, c0:c1])
            u = _dot(h, wu_ref[:, c0:c1])
            a = (g * jax.nn.sigmoid(g) * u).astype(BF16)
            part = _dot(a, wd_ref[c0:c1, :])
            acc = part if acc is None else acc + part
        f = acc.reshape(bb, sub_t, d)
        y = alpha * x + (0.5 * (1.0 + gt_ref[...])) * f
        o_ref[:, t0:t0 + sub_t, :] = _layer_norm(y, g_ref[...], b_ref[...])


def _ffn(x, mods, chunk0, wg, wu, wd, widx, ln_g, ln_b, lnidx, *, alpha, bb, tt, sub_t=None, ff_chunk=512):
    b, t, d = x.shape
    dff = wg.shape[-1]
    l, i = widx
    sub_t = tt if sub_t is None else sub_t
    assert tt % sub_t == 0
    kern = functools.partial(_ffn_kernel, alpha=alpha, ff_chunk=ff_chunk, sub_t=sub_t)
    return pl.pallas_call(
        kern,
        grid=(b // bb, t // tt),
        in_specs=[
            pl.BlockSpec((bb, tt, d), lambda bi, ti: (bi, ti, 0)),
            _mod_spec(bb, d, chunk0 + 0),
            _mod_spec(bb, d, chunk0 + 1),
            _mod_spec(bb, d, chunk0 + 2),
            _resident((None, None, d, dff), lambda bi, ti: (l, i, 0, 0)),
            _resident((None, None, d, dff), lambda bi, ti: (l, i, 0, 0)),
            _resident((None, None, dff, d), lambda bi, ti: (l, i, 0, 0)),
            _resident((None, 1, d), lambda bi, ti: (lnidx, 0, 0)),
            _resident((None, 1, d), lambda bi, ti: (lnidx, 0, 0)),
        ],
        out_specs=pl.BlockSpec((bb, tt, d), lambda bi, ti: (bi, ti, 0)),
        out_shape=jax.ShapeDtypeStruct((b, t, d), F32),
        compiler_params=_cparams(2),
        name="ffn",
    )(x, mods, mods, mods, wg, wu, wd, ln_g, ln_b)


def _conv_kernel(x_ref, sh_ref, sc_ref, gt_ref, prev_ref, w1_ref, b1_ref, dw_ref, dwb_ref,
                 cg_ref, cb_ref, w2_ref, b2_ref, g_ref, b_ref, o_ref, st_ref,
                 u_scr, ut_scr, ue_scr, uo_scr, zt_scr,
                 *, alpha, conv_k, tblock):
    bb, tt, d = x_ref.shape
    ti = pl.program_id(1)
    keep = conv_k - 1
    off = CONV_HALO - keep
    nl = d // LANES

    @pl.when(ti == 0)
    def _():
        u_scr[:, 0:off, :] = jnp.zeros((bb, off, d), F32)
        u_scr[:, off:CONV_HALO, :] = prev_ref[...]

    x = x_ref[...]
    h = (x * (1.0 + sc_ref[...]) + sh_ref[...]).reshape(bb * tt, d).astype(BF16)
    u2 = _dot(h, w1_ref[...]) + b1_ref[...]
    u = u2[:, :d] * jax.nn.sigmoid(u2[:, d:])
    u_scr[:, CONV_HALO:CONV_HALO + tt, :] = u.reshape(bb, tt, d)

    n_steps = CONV_HALO + tt
    for bi in range(bb):
        for j in range(nl):
            ut_scr[bi, pl.ds(j, n_steps, stride=nl), :] = u_scr[bi, :, j * LANES:(j + 1) * LANES]
        ue_scr[bi] = ut_scr[bi].astype(BF16)
        uo_scr[bi] = ut_scr[bi, nl:(n_steps - 1) * nl, :].astype(BF16)
    n_blocks = tt // tblock

    def block_body(it, carry):
        bi = it // n_blocks
        base = pl.multiple_of((it % n_blocks) * (tblock * nl), tblock * nl)
        accs = [jnp.zeros((2 * nl, LANES), F32) for _ in range(tblock // 2)]
        for k in range(conv_k):
            w = dw_ref[k].astype(F32)
            shift = off + k
            src, first = (ue_scr, shift) if shift % 2 == 0 else (uo_scr, shift - 1)
            for i in range(tblock // 2):
                u2 = src[bi, pl.ds(base + (2 * i + first) * nl, 2 * nl), :]
                accs[i] = accs[i] + w * u2.astype(F32)
        for i in range(tblock // 2):
            zt_scr[bi, pl.ds(base + 2 * i * nl, 2 * nl), :] = accs[i]
        return carry

    lax.fori_loop(0, bb * n_blocks, block_body, 0)

    st_ref[...] = u_scr[:, tt + off:tt + CONV_HALO, :]
    if tt >= CONV_HALO:
        u_scr[:, 0:CONV_HALO, :] = u_scr[:, tt:tt + CONV_HALO, :]

    z = jnp.stack([
        jnp.concatenate([zt_scr[bi, pl.ds(j, tt, stride=nl), :] for j in range(nl)], axis=1)
        for bi in range(bb)])
    z = z + dwb_ref[...]
    z = _layer_norm(z, cg_ref[...], cb_ref[...])
    z = (z * jax.nn.sigmoid(z)).reshape(bb * tt, d).astype(BF16)
    mix = (_dot(z, w2_ref[...]) + b2_ref[...]).reshape(bb, tt, d)
    y = alpha * x + (1.0 + gt_ref[...]) * mix
    o_ref[...] = _layer_norm(y, g_ref[...], b_ref[...])


def _conv(x, mods, chunk0, prev, w1, b1, dw, dwb, cg, cb, w2, b2, ln_g, ln_b, lnidx,
          *, alpha, bb, tt):
    b, t, d = x.shape
    conv_k = dw.shape[0]
    keep = conv_k - 1
    assert keep <= CONV_HALO and (t == tt or tt >= CONV_HALO)
    nl = d // LANES
    assert nl == SUBLANES
    tblock = min(tt, CONV_TBLOCK)
    assert tblock % 2 == 0 and tt % tblock == 0
    dw_tiles = dw.reshape(conv_k, nl, LANES)
    dw_pairs = jnp.concatenate([dw_tiles, dw_tiles], axis=1).astype(BF16)
    kern = functools.partial(_conv_kernel, alpha=alpha, conv_k=conv_k, tblock=tblock)
    vec = lambda: _resident((1, d), lambda bi, ti: (0, 0))
    return pl.pallas_call(
        kern,
        grid=(b // bb, t // tt),
        in_specs=[
            pl.BlockSpec((bb, tt, d), lambda bi, ti: (bi, ti, 0)),
            _mod_spec(bb, d, chunk0 + 0),
            _mod_spec(bb, d, chunk0 + 1),
            _mod_spec(bb, d, chunk0 + 2),
            pl.BlockSpec((bb, keep, d), lambda bi, ti: (bi, 0, 0)),
            _resident((d, 2 * d), lambda bi, ti: (0, 0)),
            _resident((1, 2 * d), lambda bi, ti: (0, 0)),
            _resident((conv_k, 2 * nl, LANES), lambda bi, ti: (0, 0, 0)),
            vec(), vec(), vec(),
            _resident((d, d), lambda bi, ti: (0, 0)),
            vec(),
            _resident((None, 1, d), lambda bi, ti: (lnidx, 0, 0)),
            _resident((None, 1, d), lambda bi, ti: (lnidx, 0, 0)),
        ],
        out_specs=[
            pl.BlockSpec((bb, tt, d), lambda bi, ti: (bi, ti, 0)),
            pl.BlockSpec((bb, keep, d), lambda bi, ti: (bi, 0, 0)),
        ],
        out_shape=[
            jax.ShapeDtypeStruct((b, t, d), F32),
            jax.ShapeDtypeStruct((b, keep, d), F32),
        ],
        scratch_shapes=[
            pltpu.VMEM((bb, CONV_HALO + tt, d), F32),
            pltpu.VMEM((bb, (CONV_HALO + tt) * nl, LANES), F32),
            pltpu.VMEM((bb, (CONV_HALO + tt) * nl, LANES), BF16),
            pltpu.VMEM((bb, (CONV_HALO + tt - 2) * nl, LANES), BF16),
            pltpu.VMEM((bb, tt * nl, LANES), F32),
        ],
        compiler_params=_cparams(2),
        name="conv",
    )(x, mods, mods, mods, prev, w1, b1, dw_pairs, dwb, cg, cb, w2, b2, ln_g, ln_b)


def _log_sigmoid(f):
    return jnp.minimum(f, 0.0) - jnp.log1p(jnp.exp(-jnp.abs(f)))


def _tri_ones(n):
    row = lax.broadcasted_iota(jnp.int32, (n, n), 0)
    col = lax.broadcasted_iota(jnp.int32, (n, n), 1)
    return jnp.where(row <= col, 1.0, 0.0).astype(BF16)


def _running_sum_lanes(f, tri, carry):
    rows = f.shape[0]
    hi, mid, lo = _split3(f)
    c3 = _dot(jnp.concatenate([hi, mid, lo], axis=0), tri)
    return c3[0:rows] + c3[rows:2 * rows] + c3[2 * rows:3 * rows] + carry


def _kv_rows_kernel(x_ref, sh_ref, sc_ref, wkv_ref, wf_ref, bf_ref, k_ref, v_ref, lf_ref, kb_ref, vb_ref):
    bb, tt, d = x_ref.shape
    nh = lf_ref.shape[-1]
    x = x_ref[...]
    h = (x * (1.0 + sc_ref[...]) + sh_ref[...]).reshape(bb * tt, d).astype(BF16)
    kv = _dot(h, wkv_ref[...])
    k = kv[:, :d].reshape(bb, tt, d)
    v = kv[:, d:].reshape(bb, tt, d)
    k_ref[...] = k
    v_ref[...] = v
    kb_ref[...] = k.astype(BF16)
    vb_ref[...] = v.astype(BF16)
    lf = _log_sigmoid(_dot(h, wf_ref[...]) + bf_ref[...])
    lf_ref[...] = lf[:, :nh].reshape(bb, tt, nh)


def _kv_rows(x, modskv, wkv, wf, bf, nh, *, bb, tt):
    b, t, d = x.shape
    blk = lambda w: pl.BlockSpec((bb, tt, w), lambda bi, ti: (bi, ti, 0))
    return pl.pallas_call(
        _kv_rows_kernel,
        grid=(b // bb, t // tt),
        in_specs=[
            blk(d),
            _mod_spec(bb, d, 0),
            _mod_spec(bb, d, 1),
            _resident((d, 2 * d), lambda bi, ti: (0, 0)),
            _resident((d, LANES), lambda bi, ti: (0, 0)),
            _resident((1, LANES), lambda bi, ti: (0, 0)),
        ],
        out_specs=[blk(d), blk(d), blk(nh), blk(d), blk(d)],
        out_shape=[
            jax.ShapeDtypeStruct((b, t, d), F32),
            jax.ShapeDtypeStruct((b, t, d), F32),
            jax.ShapeDtypeStruct((b, t, nh), F32),
            jax.ShapeDtypeStruct((b, t, d), BF16),
            jax.ShapeDtypeStruct((b, t, d), BF16),
        ],
        compiler_params=_cparams(2),
        name="kv_rows",
    )(x, modskv, modskv, wkv, wf, bf)


def _kv_cols_kernel(x_ref, sh_ref, sc_ref, wkvt_ref, wft_ref, bft_ref,
                    kt_ref, vt_ref, lft_ref, kbt_ref, vbt_ref, cum_ref, carry_scr):
    _, tt, d = x_ref.shape
    nh = lft_ref.shape[1]
    x = x_ref[0]
    h = (x * (1.0 + sc_ref[0]) + sh_ref[0]).astype(BF16)
    kvt = _dot_nt(wkvt_ref[...], h)
    kt = kvt[:d]
    vt = kvt[d:]
    kt_ref[0] = kt
    vt_ref[0] = vt
    kbt_ref[0, 0] = kt.astype(BF16)
    vbt_ref[0, 0] = vt.astype(BF16)
    lft = _log_sigmoid(_dot_nt(wft_ref[...], h) + bft_ref[...])
    lft_ref[0] = lft[0:nh]

    @pl.when(pl.program_id(1) == 0)
    def _():
        carry_scr[...] = jnp.zeros(carry_scr.shape, F32)

    cum = _running_sum_lanes(lft, _tri_ones(tt), carry_scr[:, 0:1])
    cum_ref[0, 0] = cum[0:nh]
    carry_scr[...] = jnp.broadcast_to(cum[:, tt - 1:tt], carry_scr.shape)


def _kv_cols(x, modskv, wkvt, wft, bft, nh, *, tt):
    b, t, d = x.shape
    nt = t // tt
    cols = lambda rows: pl.BlockSpec((1, rows, tt), lambda bi, ti: (bi, 0, ti))
    chunk = lambda rows: pl.BlockSpec((1, 1, rows, tt), lambda bi, ti: (bi, ti, 0, 0))
    return pl.pallas_call(
        _kv_cols_kernel,
        grid=(b, nt),
        in_specs=[
            pl.BlockSpec((1, tt, d), lambda bi, ti: (bi, ti, 0)),
            _mod_spec(1, d, 0),
            _mod_spec(1, d, 1),
            _resident((2 * d, d), lambda bi, ti: (0, 0)),
            _resident((LANES, d), lambda bi, ti: (0, 0)),
            _resident((LANES, 1), lambda bi, ti: (0, 0)),
        ],
        out_specs=[cols(d), cols(d), cols(nh), chunk(d), chunk(d), chunk(nh)],
        out_shape=[
            jax.ShapeDtypeStruct((b, d, t), F32),
            jax.ShapeDtypeStruct((b, d, t), F32),
            jax.ShapeDtypeStruct((b, nh, t), F32),
            jax.ShapeDtypeStruct((b, nt, d, tt), BF16),
            jax.ShapeDtypeStruct((b, nt, d, tt), BF16),
            jax.ShapeDtypeStruct((b, nt, nh, tt), F32),
        ],
        scratch_shapes=[pltpu.VMEM((LANES, LANES), F32)],
        compiler_params=_cparams(2),
        name="kv_cols",
    )(x, modskv, modskv, wkvt, wft, bft)


def _cumsum_kernel(lf_ref, o_ref, *, tc):
    _, nh, t = lf_ref.shape
    tri = _tri_ones(tc)
    carry = jnp.zeros((nh, 1), F32)
    for c0 in range(0, t, tc):
        n = min(tc, t - c0)
        cum = _running_sum_lanes(lf_ref[0, :, c0:c0 + n], tri[0:n, 0:n], carry)
        o_ref[0, :, c0:c0 + n] = cum
        carry = cum[:, n - 1:n]


def _cumsum_heads(lft, *, tc):
    b, nh, t = lft.shape
    assert nh % 16 == 0 and t % LANES == 0
    return pl.pallas_call(
        functools.partial(_cumsum_kernel, tc=tc),
        grid=(b,),
        in_specs=[pl.BlockSpec((1, nh, t), lambda bi: (bi, 0, 0))],
        out_specs=pl.BlockSpec((1, nh, t), lambda bi: (bi, 0, 0)),
        out_shape=jax.ShapeDtypeStruct((b, nh, t), F32),
        compiler_params=_cparams(1),
        name="cumsum",
    )(lft)


def _pair_queries(q2):
    lane = lax.broadcasted_iota(jnp.int32, q2.shape, 1)
    first = lane < (LANES // HEAD_PAIR)
    zero = jnp.zeros_like(q2)
    return jnp.concatenate([jnp.where(first, q2, zero), jnp.where(first, zero, q2)], axis=0)


def _pair_causal(tq, tk):
    row = lax.broadcasted_iota(jnp.int32, (2 * tq, tk), 0)
    col = lax.broadcasted_iota(jnp.int32, (2 * tq, tk), 1)
    return col <= jnp.where(row >= tq, row - tq, row)


def _pair_bias(b0, b1, tq, mult):
    n = b0.shape[-1]
    return jnp.concatenate([jnp.broadcast_to(b0 * (-mult), (tq, n)),
                            jnp.broadcast_to(b1 * (-mult), (tq, n))], axis=0)


def _attn_prompt_kernel(x_ref, sh_ref, sc_ref, gt_ref, wq_ref, wo_ref, k_ref, v_ref, cum_ref,
                        g_ref, b_ref, o_ref, qq_scr, m_scr, acc_scr, att_scr,
                        *, alpha, scale):
    _, tq, d = x_ref.shape
    tk = tq
    n_pair = d // LANES
    qi = pl.program_id(1)
    x = x_ref[0]
    h = (x * (1.0 + sc_ref[0]) + sh_ref[0]).astype(BF16)
    q = (_dot(h, wq_ref[...]) * (scale * LOG2E)).astype(BF16)
    for pair in range(n_pair):
        qq_scr[pair] = _pair_queries(q[:, pair * LANES:(pair + 1) * LANES])
    n_grp = m_scr.shape[0]
    ones = jnp.ones((LANES, tk), BF16)
    lane = lax.broadcasted_iota(jnp.int32, (tq, LANES), 1)

    def key_tile(kj, pair0, masked):
        for g in range(n_grp):
            pair = pair0 + g
            chans = pl.ds(pair * LANES, LANES)
            s = _dot(qq_scr[pair], k_ref[0, kj, chans, :])
            s = s + _pair_bias(cum_ref[0, kj, pl.ds(HEAD_PAIR * pair, 1), :],
                               cum_ref[0, kj, pl.ds(HEAD_PAIR * pair + 1, 1), :], tq, LOG2E)
            if masked:
                s = jnp.where(_pair_causal(tq, tk), s, NEG_BIG)
            m_old = m_scr[g]
            m_new = jnp.maximum(m_old, jnp.broadcast_to(jnp.max(s, axis=-1, keepdims=True), m_old.shape))
            a = jnp.exp2(m_old - m_new)
            p = jnp.exp2(s - jnp.concatenate([m_new] * (tk // LANES), axis=1))
            v_ext = jnp.concatenate([v_ref[0, kj, chans, :], ones], axis=0)
            acc_scr[g] = jnp.concatenate([a, a], axis=1) * acc_scr[g] + _dot_nt(p.astype(BF16), v_ext)
            m_scr[g] = m_new

    for pair0 in range(0, n_pair, n_grp):
        m_scr[...] = jnp.full(m_scr.shape, NEG_BIG, F32)
        acc_scr[...] = jnp.zeros(acc_scr.shape, F32)

        def full_tile(kj, carry, pair0=pair0):
            key_tile(kj, pair0, False)
            return carry

        lax.fori_loop(0, qi, full_tile, 0)
        key_tile(qi, pair0, True)
        for g in range(n_grp):
            o = acc_scr[g, :, 0:LANES] / acc_scr[g, :, LANES:2 * LANES]
            o = jnp.where(lane < (LANES // HEAD_PAIR), o[0:tq], o[tq:2 * tq])
            att_scr[:, (pair0 + g) * LANES:(pair0 + g + 1) * LANES] = o.astype(BF16)

    mix = _dot(att_scr[...], wo_ref[...])
    y = alpha * x + (1.0 + gt_ref[0]) * mix
    o_ref[0] = _layer_norm(y, g_ref[...], b_ref[...])


def _attn_prompt(x, mods, chunk0, wq, wo, kbt, vbt, cum, ln_g, ln_b, lnidx, *, alpha, tq):
    b, t, d = x.shape
    nt, nh = cum.shape[1], cum.shape[2]
    assert kbt.shape == (b, nt, d, tq) and nt * tq == t
    scale = float(d // nh) ** -0.5
    grp = min(ATTN_GROUP, d // LANES)
    assert (d // LANES) % grp == 0
    kern = functools.partial(_attn_prompt_kernel, alpha=alpha, scale=scale)
    return pl.pallas_call(
        kern,
        grid=(b, t // tq),
        in_specs=[
            pl.BlockSpec((1, tq, d), lambda bi, qi: (bi, qi, 0)),
            _mod_spec(1, d, chunk0 + 0),
            _mod_spec(1, d, chunk0 + 1),
            _mod_spec(1, d, chunk0 + 2),
            _resident((d, d), lambda bi, qi: (0, 0)),
            _resident((d, d), lambda bi, qi: (0, 0)),
            pl.BlockSpec((1, nt, d, tq), lambda bi, qi: (bi, 0, 0, 0)),
            pl.BlockSpec((1, nt, d, tq), lambda bi, qi: (bi, 0, 0, 0)),
            pl.BlockSpec((1, nt, nh, tq), lambda bi, qi: (bi, 0, 0, 0)),
            _resident((None, 1, d), lambda bi, qi: (lnidx, 0, 0)),
            _resident((None, 1, d), lambda bi, qi: (lnidx, 0, 0)),
        ],
        out_specs=pl.BlockSpec((1, tq, d), lambda bi, qi: (bi, qi, 0)),
        out_shape=jax.ShapeDtypeStruct((b, t, d), F32),
        scratch_shapes=[
            pltpu.VMEM((d // LANES, 2 * tq, LANES), BF16),
            pltpu.VMEM((grp, 2 * tq, LANES), F32),
            pltpu.VMEM((grp, 2 * tq, 2 * LANES), F32),
            pltpu.VMEM((tq, d), BF16),
        ],
        compiler_params=_cparams(2),
        name="attn_prompt",
    )(x, mods, mods, mods, wq, wo, kbt, vbt, cum, ln_g, ln_b)


def _attn_sample_kernel(x_ref, sh_ref, sc_ref, gt_ref, wq_ref, wo_ref, ck_ref, cv_ref, nk_ref,
                        nv_ref, cum_ref, g_ref, b_ref, o_ref, att_scr, *, alpha, scale):
    _, tq, d = x_ref.shape
    past = ck_ref.shape[2]
    tn = nk_ref.shape[1]
    x = x_ref[0]
    h = (x * (1.0 + sc_ref[0]) + sh_ref[0]).astype(BF16)
    q = (_dot(h, wq_ref[...]) * scale).astype(BF16)
    causal = _pair_causal(tq, tn)
    lane = lax.broadcasted_iota(jnp.int32, (tq, LANES), 1)

    def pair_bias(pair, lo, n):
        h0 = HEAD_PAIR * pair
        return _pair_bias(cum_ref[0, h0:h0 + 1, lo:lo + n], cum_ref[0, h0 + 1:h0 + 2, lo:lo + n], tq, 1.0)

    for pair in range(d // LANES):
        lanes = pl.ds(pair * LANES, LANES)
        qq = _pair_queries(q[:, pair * LANES:(pair + 1) * LANES])
        s_p = _dot(qq, ck_ref[0, lanes, :].astype(BF16)) + pair_bias(pair, 0, past)
        s_n = _dot_nt(qq, nk_ref[0, :, lanes]) + pair_bias(pair, past, tn)
        s_n = jnp.where(causal, s_n, NEG_BIG)
        m = jnp.maximum(jnp.max(s_p, axis=-1, keepdims=True), jnp.max(s_n, axis=-1, keepdims=True))
        p_p = jnp.exp(s_p - m)
        p_n = jnp.exp(s_n - m)
        l = jnp.sum(p_p, axis=-1, keepdims=True) + jnp.sum(p_n, axis=-1, keepdims=True)
        o = (_dot_nt(p_p.astype(BF16), cv_ref[0, lanes, :].astype(BF16))
             + _dot(p_n.astype(BF16), nv_ref[0, :, lanes]))
        o = o / l
        att_scr[:, lanes] = jnp.where(lane < (LANES // HEAD_PAIR), o[0:tq], o[tq:2 * tq]).astype(BF16)

    mix = _dot(att_scr[...], wo_ref[...])
    y = alpha * x + (1.0 + gt_ref[0]) * mix
    o_ref[0] = _layer_norm(y, g_ref[...], b_ref[...])


def _attn_sample(x, mods, chunk0, wq, wo, ckt, cvt, nk, nv, cum, ln_g, ln_b, lnidx, *, alpha):
    b, tq, d = x.shape
    past = ckt.shape[2]
    tn = nk.shape[1]
    nh = cum.shape[1]
    scale = float(d // nh) ** -0.5
    kern = functools.partial(_attn_sample_kernel, alpha=alpha, scale=scale)
    per_b = lambda shape: pl.BlockSpec(shape, lambda bi: (bi,) + (0,) * (len(shape) - 1))
    mod = lambda c: pl.BlockSpec((None, 1, 1, d), lambda bi: (c, bi, 0, 0))
    return pl.pallas_call(
        kern,
        grid=(b,),
        in_specs=[
            per_b((1, tq, d)),
            mod(chunk0 + 0), mod(chunk0 + 1), mod(chunk0 + 2),
            _resident((d, d), lambda bi: (0, 0)),
            _resident((d, d), lambda bi: (0, 0)),
            per_b((1, d, past)), per_b((1, d, past)),
            per_b((1, tn, d)), per_b((1, tn, d)),
            per_b((1, nh, cum.shape[2])),
            _resident((None, 1, d), lambda bi: (lnidx, 0, 0)),
            _resident((None, 1, d), lambda bi: (lnidx, 0, 0)),
        ],
        out_specs=per_b((1, tq, d)),
        out_shape=jax.ShapeDtypeStruct((b, tq, d), F32),
        scratch_shapes=[pltpu.VMEM((tq, d), BF16)],
        compiler_params=_cparams(1),
        name="attn_sample",
    )(x, mods, mods, mods, wq, wo, ckt, cvt, nk, nv, cum, ln_g, ln_b)


def _tiles(t):
    return (512 if t % 512 == 0 else t), (512 if t % 512 == 0 else t)


def _trunk(x, conv_prev, kv_cache, mods, modskv, W, *, batched):
    b, t, d = x.shape
    depth = W["n_layers"]
    n_a = W["n_a"]
    nh = W["n_heads"]
    alpha = float((2 * depth) ** 0.25)
    tt, tq = _tiles(t)
    bb = b if batched else 1
    tt = t if batched else tt
    new_conv = []
    k_new = v_new = lf_new = None
    hd = d // nh
    for l in range(depth):
        ch = l * 9
        if l == n_a and kv_cache is None:
            kt, vt, lft, kbt, vbt, cum = _kv_cols(x, modskv, W["wkvt"], W["wft"], W["bft"], nh, tt=tq)
            k_new = kt.reshape(b, nh, hd, t).transpose(0, 3, 1, 2)
            v_new = vt.reshape(b, nh, hd, t).transpose(0, 3, 1, 2)
            lf_new = lft.transpose(0, 2, 1)
        elif l == n_a:
            k, v, lf_new, kb, vb = _kv_rows(x, modskv, W["wkv"], W["wf"], W["bf"], nh, bb=bb, tt=tt)
            k_new = k.reshape(b, t, nh, hd)
            v_new = v.reshape(b, t, nh, hd)
        x = _ffn(x, mods, ch + 0, W["wg"], W["wu"], W["wd"], (l, 0), W["ln_g"], W["ln_b"],
                 l * 3 + 0, alpha=alpha, bb=bb, tt=tt)
        if l < n_a:
            x, st = _conv(x, mods, ch + 3, conv_prev[l], W["pw1_w"][l], W["pw1_b"][l],
                          W["dw_w"][l], W["dw_b"][l], W["cln_g"][l], W["cln_b"][l],
                          W["pw2_w"][l], W["pw2_b"][l], W["ln_g"], W["ln_b"], l * 3 + 1,
                          alpha=alpha, bb=bb, tt=tt)
            new_conv.append(st)
        else:
            j = l - n_a
            if kv_cache is None:
                x = _attn_prompt(x, mods, ch + 3, W["wq"][j], W["wo"][j], kbt, vbt, cum,
                                 W["ln_g"], W["ln_b"], l * 3 + 1, alpha=alpha, tq=tq)
            else:
                ck, cv, clf = kv_cache
                past = ck.shape[1]
                assert past % LANES == 0 and t <= LANES
                ckt = ck.transpose(0, 2, 3, 1).reshape(b, d, past)
                cvt = cv.transpose(0, 2, 3, 1).reshape(b, d, past)
                lft_new = jnp.pad(lf_new.transpose(0, 2, 1), ((0, 0), (0, 0), (0, LANES - t)))
                cum = _cumsum_heads(jnp.concatenate([clf.transpose(0, 2, 1), lft_new], axis=2), tc=512)
                nk = jnp.pad(kb, ((0, 0), (0, LANES - t), (0, 0)))
                nv = jnp.pad(vb, ((0, 0), (0, LANES - t), (0, 0)))
                x = _attn_sample(x, mods, ch + 3, W["wq"][j], W["wo"][j], ckt, cvt, nk, nv,
                                 cum, W["ln_g"], W["ln_b"], l * 3 + 1, alpha=alpha)
        x = _ffn(x, mods, ch + 6, W["wg"], W["wu"], W["wd"], (l, 1), W["ln_g"], W["ln_b"],
                 l * 3 + 2, alpha=alpha, bb=bb, tt=tt)
    return x, jnp.stack(new_conv), k_new, v_new, lf_new


def kernel(x_prompt, x_sample, cache_conv, cache_k, cache_v, cache_logf, c_prompt, c_sample,
           w_ada, b_ada, ln_g, ln_b, ffn_w_gate, ffn_w_up, ffn_w_down,
           pw1_w, pw1_b, dw_w, dw_b, conv_ln_g, conv_ln_b, pw2_w, pw2_b,
           w_ada_kv, b_ada_kv, w_k, w_v, w_f, b_f, w_q, w_o):
    bp, tp, d = x_prompt.shape
    bs, ts, _ = x_sample.shape
    depth = w_ada.shape[0]
    n_a = pw1_w.shape[0]
    nh = w_f.shape[1]
    assert d % LANES == 0 and d // nh == LANES // HEAD_PAIR and nh <= LANES
    assert ts % 16 == 0

    c_all = jnp.concatenate([c_prompt, c_sample], axis=0)
    mods = _ada(c_all, w_ada, b_ada)
    modskv = _ada(c_all, w_ada_kv[None], b_ada_kv[None])

    W = dict(
        n_layers=depth, n_a=n_a, n_heads=nh,
        wg=ffn_w_gate.astype(BF16), wu=ffn_w_up.astype(BF16), wd=ffn_w_down.astype(BF16),
        ln_g=ln_g.reshape(depth * 3, 1, d), ln_b=ln_b.reshape(depth * 3, 1, d),
        pw1_w=pw1_w.astype(BF16), pw1_b=pw1_b.reshape(n_a, 1, 2 * d),
        dw_w=dw_w, dw_b=dw_b.reshape(n_a, 1, d),
        cln_g=conv_ln_g.reshape(n_a, 1, d), cln_b=conv_ln_b.reshape(n_a, 1, d),
        pw2_w=pw2_w.astype(BF16), pw2_b=pw2_b.reshape(n_a, 1, d),
        wq=w_q.astype(BF16), wo=w_o.astype(BF16),
    )
    W["wkv"] = jnp.concatenate([w_k, w_v], axis=1).astype(BF16)
    W["wf"] = jnp.pad(w_f, ((0, 0), (0, LANES - nh))).astype(BF16)
    W["bf"] = jnp.pad(b_f, (0, LANES - nh)).reshape(1, LANES)
    W["wkvt"], W["wft"], W["bft"] = W["wkv"].T, W["wf"].T, W["bf"].T

    zero_conv = jnp.zeros((n_a, bp, dw_w.shape[1] - 1, d), x_prompt.dtype)
    y_p, conv_p, k_p, v_p, lf_p = _trunk(x_prompt, zero_conv, None, mods[:, :bp], modskv[:, :bp], W, batched=False)
    y_s, conv_s, k_s, v_s, lf_s = _trunk(x_sample, cache_conv, (cache_k, cache_v, cache_logf),
                                         mods[:, bp:], modskv[:, bp:], W, batched=True)
    return (y_p, y_s, conv_p, k_p, v_p, lf_p, conv_s, k_s, v_s, lf_s)
```

```python
import functools

import jax
import jax.numpy as jnp
from jax import lax
from jax.experimental import pallas as pl
from jax.experimental.pallas import tpu as pltpu

F32 = jnp.float32
BF16 = jnp.bfloat16

LN_EPS = 1e-5
LANES = 128
SUBLANES = 8
HEAD_PAIR = 2
CONV_HALO = 32
VMEM_LIMIT_BYTES = 56 * 1024 * 1024
FFN_CHUNK = 512
CONV_TBLOCK = 32
ATTN_GROUP = 8
NEG_BIG = -1e30
LOG2E = 1.4426950408889634


def _cparams(n_grid):
    return pltpu.CompilerParams(
        dimension_semantics=("arbitrary",) * n_grid,
        vmem_limit_bytes=VMEM_LIMIT_BYTES,
    )


def _resident(shape, index_map):
    return pl.BlockSpec(shape, index_map, pipeline_mode=pl.Buffered(1))


def _layer_norm(y, g, b):
    mu = jnp.mean(y, axis=-1, keepdims=True)
    yc = y - mu
    var = jnp.mean(yc * yc, axis=-1, keepdims=True)
    return yc * lax.rsqrt(var + LN_EPS) * g + b


def _dot(a, b):
    return jnp.dot(a, b, preferred_element_type=F32)


def _dot_nt(a, b):
    return lax.dot_general(a, b, (((1,), (1,)), ((), ())), preferred_element_type=F32)


def _split3(x):
    hi = x.astype(BF16)
    r1 = x - hi.astype(F32)
    mid = r1.astype(BF16)
    lo = (r1 - mid.astype(F32)).astype(BF16)
    return hi, mid, lo


def _ada_kernel(c_ref, w_ref, b_ref, o_ref):
    c = c_ref[...]
    ca = (c * jax.nn.sigmoid(c)).astype(BF16)
    o_ref[...] = _dot(ca, w_ref[...].astype(BF16)) + b_ref[...]


def _ada(c, w, b):
    bc, d = c.shape
    nl, _, n = w.shape
    nch = n // d
    out = pl.pallas_call(
        _ada_kernel,
        grid=(nl, nch),
        in_specs=[
            _resident((bc, d), lambda l, j: (0, 0)),
            pl.BlockSpec((None, d, d), lambda l, j: (l, 0, j)),
            pl.BlockSpec((None, 1, d), lambda l, j: (l * nch + j, 0, 0)),
        ],
        out_specs=pl.BlockSpec((None, bc, d), lambda l, j: (l * nch + j, 0, 0)),
        out_shape=jax.ShapeDtypeStruct((nl * nch, bc, d), F32),
        compiler_params=_cparams(2),
        name="ada",
    )(c, w, b.reshape(nl * nch, 1, d))
    return out.reshape(nl * nch, bc, 1, d)


def _mod_spec(bb, d, chunk):
    return pl.BlockSpec((None, bb, 1, d), lambda b, t: (chunk, b, 0, 0))


def _ffn_kernel(x_ref, sh_ref, sc_ref, gt_ref, wg_ref, wu_ref, wd_ref, g_ref, b_ref, o_ref,
                *, alpha):
    bb, tt, d = x_ref.shape
    dff = wg_ref.shape[1]
    x = x_ref[...]
    h = (x * (1.0 + sc_ref[...]) + sh_ref[...]).reshape(bb * tt, d).astype(BF16)
    acc = None
    for c0 in range(0, dff, FFN_CHUNK):
        c1 = min(c0 + FFN_CHUNK, dff)
        g = _dot(h, wg_ref[:, c0:c1])
        u = _dot(h, wu_ref[:, c0:c1])
        a = (g * jax.nn.sigmoid(g) * u).astype(BF16)
        part = _dot(a, wd_ref[c0:c1, :])
        acc = part if acc is None else acc + part
    f = acc.reshape(bb, tt, d)
    y = alpha * x + (0.5 * (1.0 + gt_ref[...])) * f
    o_ref[...] = _layer_norm(y, g_ref[...], b_ref[...])


def _ffn(x, mods, chunk0, wg, wu, wd, widx, ln_g, ln_b, lnidx, *, alpha, bb, tt):
    b, t, d = x.shape
    dff = wg.shape[-1]
    l, i = widx
    kern = functools.partial(_ffn_kernel, alpha=alpha)
    return pl.pallas_call(
        kern,
        grid=(b // bb, t // tt),
        in_specs=[
            pl.BlockSpec((bb, tt, d), lambda bi, ti: (bi, ti, 0)),
            _mod_spec(bb, d, chunk0 + 0),
            _mod_spec(bb, d, chunk0 + 1),
            _mod_spec(bb, d, chunk0 + 2),
            _resident((None, None, d, dff), lambda bi, ti: (l, i, 0, 0)),
            _resident((None, None, d, dff), lambda bi, ti: (l, i, 0, 0)),
            _resident((None, None, dff, d), lambda bi, ti: (l, i, 0, 0)),
            _resident((None, 1, d), lambda bi, ti: (lnidx, 0, 0)),
            _resident((None, 1, d), lambda bi, ti: (lnidx, 0, 0)),
        ],
        out_specs=pl.BlockSpec((bb, tt, d), lambda bi, ti: (bi, ti, 0)),
        out_shape=jax.ShapeDtypeStruct((b, t, d), F32),
        compiler_params=_cparams(2),
        name="ffn",
    )(x, mods, mods, mods, wg, wu, wd, ln_g, ln_b)


def _conv_kernel(x_ref, sh_ref, sc_ref, gt_ref, prev_ref, w1_ref, b1_ref, dw_ref, dwb_ref,
                 cg_ref, cb_ref, w2_ref, b2_ref, g_ref, b_ref, o_ref, st_ref,
                 u_scr, ut_scr, ue_scr, uo_scr, zt_scr,
                 *, alpha, conv_k, tblock):
    bb, tt, d = x_ref.shape
    ti = pl.program_id(1)
    keep = conv_k - 1
    off = CONV_HALO - keep
    nl = d // LANES

    @pl.when(ti == 0)
    def _():
        u_scr[:, 0:off, :] = jnp.zeros((bb, off, d), F32)
        u_scr[:, off:CONV_HALO, :] = prev_ref[...]

    x = x_ref[...]
    h = (x * (1.0 + sc_ref[...]) + sh_ref[...]).reshape(bb * tt, d).astype(BF16)
    u2 = _dot(h, w1_ref[...]) + b1_ref[...]
    u = u2[:, :d] * jax.nn.sigmoid(u2[:, d:])
    u_scr[:, CONV_HALO:CONV_HALO + tt, :] = u.reshape(bb, tt, d)

    n_steps = CONV_HALO + tt
    for bi in range(bb):
        for j in range(nl):
            ut_scr[bi, pl.ds(j, n_steps, stride=nl), :] = u_scr[bi, :, j * LANES:(j + 1) * LANES]
        ue_scr[bi] = ut_scr[bi].astype(BF16)
        uo_scr[bi] = ut_scr[bi, nl:(n_steps - 1) * nl, :].astype(BF16)
    n_blocks = tt // tblock

    def block_body(it, carry):
        bi = it // n_blocks
        base = pl.multiple_of((it % n_blocks) * (tblock * nl), tblock * nl)
        accs = [jnp.zeros((2 * nl, LANES), F32) for _ in range(tblock // 2)]
        for k in range(conv_k):
            w = dw_ref[k].astype(F32)
            shift = off + k
            src, first = (ue_scr, shift) if shift % 2 == 0 else (uo_scr, shift - 1)
            for i in range(tblock // 2):
                u2 = src[bi, pl.ds(base + (2 * i + first) * nl, 2 * nl), :]
                accs[i] = accs[i] + w * u2.astype(F32)
        for i in range(tblock // 2):
            zt_scr[bi, pl.ds(base + 2 * i * nl, 2 * nl), :] = accs[i]
        return carry

    lax.fori_loop(0, bb * n_blocks, block_body, 0)

    st_ref[...] = u_scr[:, tt + off:tt + CONV_HALO, :]
    if tt >= CONV_HALO:
        u_scr[:, 0:CONV_HALO, :] = u_scr[:, tt:tt + CONV_HALO, :]

    z = jnp.stack([
        jnp.concatenate([zt_scr[bi, pl.ds(j, tt, stride=nl), :] for j in range(nl)], axis=1)
        for bi in range(bb)])
    z = z + dwb_ref[...]
    z = _layer_norm(z, cg_ref[...], cb_ref[...])
    z = (z * jax.nn.sigmoid(z)).reshape(bb * tt, d).astype(BF16)
    mix = (_dot(z, w2_ref[...]) + b2_ref[...]).reshape(bb, tt, d)
    y = alpha * x + (1.0 + gt_ref[...]) * mix
    o_ref[...] = _layer_norm(y, g_ref[...], b_ref[...])


def _conv(x, mods, chunk0, prev, w1, b1, dw, dwb, cg, cb, w2, b2, ln_g, ln_b, lnidx,
          *, alpha, bb, tt):
    b, t, d = x.shape
    conv_k = dw.shape[0]
    keep = conv_k - 1
    assert keep <= CONV_HALO and (t == tt or tt >= CONV_HALO)
    nl = d // LANES
    assert nl == SUBLANES
    tblock = min(tt, CONV_TBLOCK)
    assert tblock % 2 == 0 and tt % tblock == 0
    dw_tiles = dw.reshape(conv_k, nl, LANES)
    dw_pairs = jnp.concatenate([dw_tiles, dw_tiles], axis=1).astype(BF16)
    kern = functools.partial(_conv_kernel, alpha=alpha, conv_k=conv_k, tblock=tblock)
    vec = lambda: _resident((1, d), lambda bi, ti: (0, 0))
    return pl.pallas_call(
        kern,
        grid=(b // bb, t // tt),
        in_specs=[
            pl.BlockSpec((bb, tt, d), lambda bi, ti: (bi, ti, 0)),
            _mod_spec(bb, d, chunk0 + 0),
            _mod_spec(bb, d, chunk0 + 1),
            _mod_spec(bb, d, chunk0 + 2),
            pl.BlockSpec((bb, keep, d), lambda bi, ti: (bi, 0, 0)),
            _resident((d, 2 * d), lambda bi, ti: (0, 0)),
            _resident((1, 2 * d), lambda bi, ti: (0, 0)),
            _resident((conv_k, 2 * nl, LANES), lambda bi, ti: (0, 0, 0)),
            vec(), vec(), vec(),
            _resident((d, d), lambda bi, ti: (0, 0)),
            vec(),
            _resident((None, 1, d), lambda bi, ti: (lnidx, 0, 0)),
            _resident((None, 1, d), lambda bi, ti: (lnidx, 0, 0)),
        ],
        out_specs=[
            pl.BlockSpec((bb, tt, d), lambda bi, ti: (bi, ti, 0)),
            pl.BlockSpec((bb, keep, d), lambda bi, ti: (bi, 0, 0)),
        ],
        out_shape=[
            jax.ShapeDtypeStruct((b, t, d), F32),
            jax.ShapeDtypeStruct((b, keep, d), F32),
        ],
        scratch_shapes=[
            pltpu.VMEM((bb, CONV_HALO + tt, d), F32),
            pltpu.VMEM((bb, (CONV_HALO + tt) * nl, LANES), F32),
            pltpu.VMEM((bb, (CONV_HALO + tt) * nl, LANES), BF16),
            pltpu.VMEM((bb, (CONV_HALO + tt - 2) * nl, LANES), BF16),
            pltpu.VMEM((bb, tt * nl, LANES), F32),
        ],
        compiler_params=_cparams(2),
        name="conv",
    )(x, mods, mods, mods, prev, w1, b1, dw_pairs, dwb, cg, cb, w2, b2, ln_g, ln_b)


def _log_sigmoid(f):
    return jnp.minimum(f, 0.0) - jnp.log1p(jnp.exp(-jnp.abs(f)))


def _tri_ones(n):
    row = lax.broadcasted_iota(jnp.int32, (n, n), 0)
    col = lax.broadcasted_iota(jnp.int32, (n, n), 1)
    return jnp.where(row <= col, 1.0, 0.0).astype(BF16)


def _running_sum_lanes(f, tri, carry):
    rows = f.shape[0]
    hi, mid, lo = _split3(f)
    c3 = _dot(jnp.concatenate([hi, mid, lo], axis=0), tri)
    return c3[0:rows] + c3[rows:2 * rows] + c3[2 * rows:3 * rows] + carry


def _kv_rows_kernel(x_ref, sh_ref, sc_ref, wkv_ref, wf_ref, bf_ref, k_ref, v_ref, lf_ref, kb_ref, vb_ref):
    bb, tt, d = x_ref.shape
    nh = lf_ref.shape[-1]
    x = x_ref[...]
    h = (x * (1.0 + sc_ref[...]) + sh_ref[...]).reshape(bb * tt, d).astype(BF16)
    kv = _dot(h, wkv_ref[...])
    k = kv[:, :d].reshape(bb, tt, d)
    v = kv[:, d:].reshape(bb, tt, d)
    k_ref[...] = k
    v_ref[...] = v
    kb_ref[...] = k.astype(BF16)
    vb_ref[...] = v.astype(BF16)
    lf = _log_sigmoid(_dot(h, wf_ref[...]) + bf_ref[...])
    lf_ref[...] = lf[:, :nh].reshape(bb, tt, nh)


def _kv_rows(x, modskv, wkv, wf, bf, nh, *, bb, tt):
    b, t, d = x.shape
    blk = lambda w: pl.BlockSpec((bb, tt, w), lambda bi, ti: (bi, ti, 0))
    return pl.pallas_call(
        _kv_rows_kernel,
        grid=(b // bb, t // tt),
        in_specs=[
            blk(d),
            _mod_spec(bb, d, 0),
            _mod_spec(bb, d, 1),
            _resident((d, 2 * d), lambda bi, ti: (0, 0)),
            _resident((d, LANES), lambda bi, ti: (0, 0)),
            _resident((1, LANES), lambda bi, ti: (0, 0)),
        ],
        out_specs=[blk(d), blk(d), blk(nh), blk(d), blk(d)],
        out_shape=[
            jax.ShapeDtypeStruct((b, t, d), F32),
            jax.ShapeDtypeStruct((b, t, d), F32),
            jax.ShapeDtypeStruct((b, t, nh), F32),
            jax.ShapeDtypeStruct((b, t, d), BF16),
            jax.ShapeDtypeStruct((b, t, d), BF16),
        ],
        compiler_params=_cparams(2),
        name="kv_rows",
    )(x, modskv, modskv, wkv, wf, bf)


def _kv_cols_kernel(x_ref, sh_ref, sc_ref, wkvt_ref, wft_ref, bft_ref,
                    kt_ref, vt_ref, lft_ref, kbt_ref, vbt_ref, cum_ref, carry_scr):
    _, tt, d = x_ref.shape
    nh = lft_ref.shape[1]
    x = x_ref[0]
    h = (x * (1.0 + sc_ref[0]) + sh_ref[0]).astype(BF16)
    kvt = _dot_nt(wkvt_ref[...], h)
    kt = kvt[:d]
    vt = kvt[d:]
    kt_ref[0] = kt
    vt_ref[0] = vt
    kbt_ref[0, 0] = kt.astype(BF16)
    vbt_ref[0, 0] = vt.astype(BF16)
    lft = _log_sigmoid(_dot_nt(wft_ref[...], h) + bft_ref[...])
    lft_ref[0] = lft[0:nh]

    @pl.when(pl.program_id(1) == 0)
    def _():
        carry_scr[...] = jnp.zeros(carry_scr.shape, F32)

    cum = _running_sum_lanes(lft, _tri_ones(tt), carry_scr[:, 0:1])
    cum_ref[0, 0] = cum[0:nh]
    carry_scr[...] = jnp.broadcast_to(cum[:, tt - 1:tt], carry_scr.shape)


def _kv_cols(x, modskv, wkvt, wft, bft, nh, *, tt):
    b, t, d = x.shape
    nt = t // tt
    cols = lambda rows: pl.BlockSpec((1, rows, tt), lambda bi, ti: (bi, 0, ti))
    chunk = lambda rows: pl.BlockSpec((1, 1, rows, tt), lambda bi, ti: (bi, ti, 0, 0))
    return pl.pallas_call(
        _kv_cols_kernel,
        grid=(b, nt),
        in_specs=[
            pl.BlockSpec((1, tt, d), lambda bi, ti: (bi, ti, 0)),
            _mod_spec(1, d, 0),
            _mod_spec(1, d, 1),
            _resident((2 * d, d), lambda bi, ti: (0, 0)),
            _resident((LANES, d), lambda bi, ti: (0, 0)),
            _resident((LANES, 1), lambda bi, ti: (0, 0)),
        ],
        out_specs=[cols(d), cols(d), cols(nh), chunk(d), chunk(d), chunk(nh)],
        out_shape=[
            jax.ShapeDtypeStruct((b, d, t), F32),
            jax.ShapeDtypeStruct((b, d, t), F32),
            jax.ShapeDtypeStruct((b, nh, t), F32),
            jax.ShapeDtypeStruct((b, nt, d, tt), BF16),
            jax.ShapeDtypeStruct((b, nt, d, tt), BF16),
            jax.ShapeDtypeStruct((b, nt, nh, tt), F32),
        ],
        scratch_shapes=[pltpu.VMEM((LANES, LANES), F32)],
        compiler_params=_cparams(2),
        name="kv_cols",
    )(x, modskv, modskv, wkvt, wft, bft)


def _cumsum_kernel(lf_ref, o_ref, *, tc):
    _, nh, t = lf_ref.shape
    tri = _tri_ones(tc)
    carry = jnp.zeros((nh, 1), F32)
    for c0 in range(0, t, tc):
        n = min(tc, t - c0)
        cum = _running_sum_lanes(lf_ref[0, :, c0:c0 + n], tri[0:n, 0:n], carry)
        o_ref[0, :, c0:c0 + n] = cum
        carry = cum[:, n - 1:n]


def _cumsum_heads(lft, *, tc):
    b, nh, t = lft.shape
    assert nh % 16 == 0 and t % LANES == 0
    return pl.pallas_call(
        functools.partial(_cumsum_kernel, tc=tc),
        grid=(b,),
        in_specs=[pl.BlockSpec((1, nh, t), lambda bi: (bi, 0, 0))],
        out_specs=pl.BlockSpec((1, nh, t), lambda bi: (bi, 0, 0)),
        out_shape=jax.ShapeDtypeStruct((b, nh, t), F32),
        compiler_params=_cparams(1),
        name="cumsum",
    )(lft)


def _pair_queries(q2):
    lane = lax.broadcasted_iota(jnp.int32, q2.shape, 1)
    first = lane < (LANES // HEAD_PAIR)
    zero = jnp.zeros_like(q2)
    return jnp.concatenate([jnp.where(first, q2, zero), jnp.where(first, zero, q2)], axis=0)


def _pair_causal(tq, tk):
    row = lax.broadcasted_iota(jnp.int32, (2 * tq, tk), 0)
    col = lax.broadcasted_iota(jnp.int32, (2 * tq, tk), 1)
    return col <= jnp.where(row >= tq, row - tq, row)


def _pair_bias(b0, b1, tq, mult):
    n = b0.shape[-1]
    return jnp.concatenate([jnp.broadcast_to(b0 * (-mult), (tq, n)),
                            jnp.broadcast_to(b1 * (-mult), (tq, n))], axis=0)


def _attn_prompt_kernel(x_ref, sh_ref, sc_ref, gt_ref, wq_ref, wo_ref, k_ref, v_ref, cum_ref,
                        g_ref, b_ref, o_ref, qq_scr, m_scr, acc_scr, att_scr,
                        *, alpha, scale):
    _, tq, d = x_ref.shape
    tk = tq
    n_pair = d // LANES
    qi = pl.program_id(1)
    x = x_ref[0]
    h = (x * (1.0 + sc_ref[0]) + sh_ref[0]).astype(BF16)
    q = (_dot(h, wq_ref[...]) * (scale * LOG2E)).astype(BF16)
    half = tq // 2
    lane = lax.broadcasted_iota(jnp.int32, (tq, LANES), 1)
    first = lane < (LANES // HEAD_PAIR)
    for pair in range(n_pair):
        q2 = q[:, pair * LANES:(pair + 1) * LANES]
        qa = jnp.where(first, q2, jnp.zeros_like(q2))
        qb = jnp.where(first, jnp.zeros_like(q2), q2)
        qq_scr[pair] = jnp.concatenate([qa[0:half], qb[0:half], qa[half:tq], qb[half:tq]], axis=0)
    n_grp = m_scr.shape[0]

    def update(g, pair, kj, r0, nrows, c0, ncols, mask):
        chans = pl.ds(pair * LANES, LANES)
        rows = pl.ds(r0, nrows)
        s = _dot(qq_scr[pair, rows, :], k_ref[0, kj, chans, c0:c0 + ncols])
        b0 = cum_ref[0, kj, pl.ds(HEAD_PAIR * pair, 1), c0:c0 + ncols] * (-LOG2E)
        b1 = cum_ref[0, kj, pl.ds(HEAD_PAIR * pair + 1, 1), c0:c0 + ncols] * (-LOG2E)
        row = lax.broadcasted_iota(jnp.int32, (nrows, ncols), 0)
        s = s + jnp.where((row & half) == 0, b0, b1)
        if mask is not None:
            s = jnp.where(mask, s, NEG_BIG)
        m_old = m_scr[g, rows, :]
        m_new = jnp.maximum(m_old, jnp.broadcast_to(jnp.max(s, axis=-1, keepdims=True), m_old.shape))
        a = jnp.exp2(m_old - m_new)
        p = jnp.exp2(s - jnp.concatenate([m_new] * (ncols // LANES), axis=1))
        v_ext = jnp.concatenate([v_ref[0, kj, chans, c0:c0 + ncols], jnp.ones((LANES, ncols), BF16)], axis=0)
        acc_scr[g, rows, :] = (jnp.concatenate([a, a], axis=1) * acc_scr[g, rows, :]
                               + _dot_nt(p.astype(BF16), v_ext))
        m_scr[g, rows, :] = m_new

    row_a = lax.broadcasted_iota(jnp.int32, (2 * tq, half), 0)
    col_a = lax.broadcasted_iota(jnp.int32, (2 * tq, half), 1)
    mask_a = col_a <= jnp.where(row_a >= tq, half, row_a & (half - 1))
    row_b = lax.broadcasted_iota(jnp.int32, (tq, half), 0)
    col_b = lax.broadcasted_iota(jnp.int32, (tq, half), 1)
    mask_b = col_b <= (row_b & (half - 1))

    for pair0 in range(0, n_pair, n_grp):
        m_scr[...] = jnp.full(m_scr.shape, NEG_BIG, F32)
        acc_scr[...] = jnp.zeros(acc_scr.shape, F32)

        def full_tile(kj, carry, pair0=pair0):
            for g in range(n_grp):
                update(g, pair0 + g, kj, 0, 2 * tq, 0, tk, None)
            return carry

        lax.fori_loop(0, qi, full_tile, 0)
        for g in range(n_grp):
            update(g, pair0 + g, qi, 0, 2 * tq, 0, half, mask_a)
        for g in range(n_grp):
            update(g, pair0 + g, qi, tq, tq, half, half, mask_b)
        for g in range(n_grp):
            o = acc_scr[g, :, 0:LANES] / acc_scr[g, :, LANES:2 * LANES]
            oa = jnp.concatenate([o[0:half], o[tq:tq + half]], axis=0)
            ob = jnp.concatenate([o[half:tq], o[tq + half:2 * tq]], axis=0)
            att_scr[:, (pair0 + g) * LANES:(pair0 + g + 1) * LANES] = jnp.where(first, oa, ob).astype(BF16)

    mix = _dot(att_scr[...], wo_ref[...])
    y = alpha * x + (1.0 + gt_ref[0]) * mix
    o_ref[0] = _layer_norm(y, g_ref[...], b_ref[...])


def _attn_prompt(x, mods, chunk0, wq, wo, kbt, vbt, cum, ln_g, ln_b, lnidx, *, alpha, tq):
    b, t, d = x.shape
    nt, nh = cum.shape[1], cum.shape[2]
    assert kbt.shape == (b, nt, d, tq) and nt * tq == t
    scale = float(d // nh) ** -0.5
    grp = min(ATTN_GROUP, d // LANES)
    assert (d // LANES) % grp == 0
    assert tq % (2 * LANES) == 0 and (tq // 2) & (tq // 2 - 1) == 0
    kern = functools.partial(_attn_prompt_kernel, alpha=alpha, scale=scale)
    return pl.pallas_call(
        kern,
        grid=(b, t // tq),
        in_specs=[
            pl.BlockSpec((1, tq, d), lambda bi, qi: (bi, qi, 0)),
            _mod_spec(1, d, chunk0 + 0),
            _mod_spec(1, d, chunk0 + 1),
            _mod_spec(1, d, chunk0 + 2),
            _resident((d, d), lambda bi, qi: (0, 0)),
            _resident((d, d), lambda bi, qi: (0, 0)),
            pl.BlockSpec((1, nt, d, tq), lambda bi, qi: (bi, 0, 0, 0)),
            pl.BlockSpec((1, nt, d, tq), lambda bi, qi: (bi, 0, 0, 0)),
            pl.BlockSpec((1, nt, nh, tq), lambda bi, qi: (bi, 0, 0, 0)),
            _resident((None, 1, d), lambda bi, qi: (lnidx, 0, 0)),
            _resident((None, 1, d), lambda bi, qi: (lnidx, 0, 0)),
        ],
        out_specs=pl.BlockSpec((1, tq, d), lambda bi, qi: (bi, qi, 0)),
        out_shape=jax.ShapeDtypeStruct((b, t, d), F32),
        scratch_shapes=[
            pltpu.VMEM((d // LANES, 2 * tq, LANES), BF16),
            pltpu.VMEM((grp, 2 * tq, LANES), F32),
            pltpu.VMEM((grp, 2 * tq, 2 * LANES), F32),
            pltpu.VMEM((tq, d), BF16),
        ],
        compiler_params=_cparams(2),
        name="attn_prompt",
    )(x, mods, mods, mods, wq, wo, kbt, vbt, cum, ln_g, ln_b)


def _attn_sample_kernel(x_ref, sh_ref, sc_ref, gt_ref, wq_ref, wo_ref, ck_ref, cv_ref, nk_ref,
                        nv_ref, cum_ref, g_ref, b_ref, o_ref, att_scr, *, alpha, scale):
    _, tq, d = x_ref.shape
    past = ck_ref.shape[2]
    tn = nk_ref.shape[1]
    x = x_ref[0]
    h = (x * (1.0 + sc_ref[0]) + sh_ref[0]).astype(BF16)
    q = (_dot(h, wq_ref[...]) * scale).astype(BF16)
    causal = _pair_causal(tq, tn)
    lane = lax.broadcasted_iota(jnp.int32, (tq, LANES), 1)

    def pair_bias(pair, lo, n):
        h0 = HEAD_PAIR * pair
        return _pair_bias(cum_ref[0, h0:h0 + 1, lo:lo + n], cum_ref[0, h0 + 1:h0 + 2, lo:lo + n], tq, 1.0)

    for pair in range(d // LANES):
        lanes = pl.ds(pair * LANES, LANES)
        qq = _pair_queries(q[:, pair * LANES:(pair + 1) * LANES])
        s_p = _dot(qq, ck_ref[0, lanes, :].astype(BF16)) + pair_bias(pair, 0, past)
        s_n = _dot_nt(qq, nk_ref[0, :, lanes]) + pair_bias(pair, past, tn)
        s_n = jnp.where(causal, s_n, NEG_BIG)
        m = jnp.maximum(jnp.max(s_p, axis=-1, keepdims=True), jnp.max(s_n, axis=-1, keepdims=True))
        p_p = jnp.exp(s_p - m)
        p_n = jnp.exp(s_n - m)
        l = jnp.sum(p_p, axis=-1, keepdims=True) + jnp.sum(p_n, axis=-1, keepdims=True)
        o = (_dot_nt(p_p.astype(BF16), cv_ref[0, lanes, :].astype(BF16))
             + _dot(p_n.astype(BF16), nv_ref[0, :, lanes]))
        o = o / l
        att_scr[:, lanes] = jnp.where(lane < (LANES // HEAD_PAIR), o[0:tq], o[tq:2 * tq]).astype(BF16)

    mix = _dot(att_scr[...], wo_ref[...])
    y = alpha * x + (1.0 + gt_ref[0]) * mix
    o_ref[0] = _layer_norm(y, g_ref[...], b_ref[...])


def _attn_sample(x, mods, chunk0, wq, wo, ckt, cvt, nk, nv, cum, ln_g, ln_b, lnidx, *, alpha):
    b, tq, d = x.shape
    past = ckt.shape[2]
    tn = nk.shape[1]
    nh = cum.shape[1]
    scale = float(d // nh) ** -0.5
    kern = functools.partial(_attn_sample_kernel, alpha=alpha, scale=scale)
    per_b = lambda shape: pl.BlockSpec(shape, lambda bi: (bi,) + (0,) * (len(shape) - 1))
    mod = lambda c: pl.BlockSpec((None, 1, 1, d), lambda bi: (c, bi, 0, 0))
    return pl.pallas_call(
        kern,
        grid=(b,),
        in_specs=[
            per_b((1, tq, d)),
            mod(chunk0 + 0), mod(chunk0 + 1), mod(chunk0 + 2),
            _resident((d, d), lambda bi: (0, 0)),
            _resident((d, d), lambda bi: (0, 0)),
            per_b((1, d, past)), per_b((1, d, past)),
            per_b((1, tn, d)), per_b((1, tn, d)),
            per_b((1, nh, cum.shape[2])),
            _resident((None, 1, d), lambda bi: (lnidx, 0, 0)),
            _resident((None, 1, d), lambda bi: (lnidx, 0, 0)),
        ],
        out_specs=per_b((1, tq, d)),
        out_shape=jax.ShapeDtypeStruct((b, tq, d), F32),
        scratch_shapes=[pltpu.VMEM((tq, d), BF16)],
        compiler_params=_cparams(1),
        name="attn_sample",
    )(x, mods, mods, mods, wq, wo, ckt, cvt, nk, nv, cum, ln_g, ln_b)


def _tiles(t):
    return (512 if t % 512 == 0 else t), (512 if t % 512 == 0 else t)


def _trunk(x, conv_prev, kv_cache, mods, modskv, W, *, batched):
    b, t, d = x.shape
    depth = W["n_layers"]
    n_a = W["n_a"]
    nh = W["n_heads"]
    alpha = float((2 * depth) ** 0.25)
    tt, tq = _tiles(t)
    bb = b if batched else 1
    tt = t if batched else tt
    new_conv = []
    k_new = v_new = lf_new = None
    hd = d // nh
    for l in range(depth):
        ch = l * 9
        if l == n_a and kv_cache is None:
            kt, vt, lft, kbt, vbt, cum = _kv_cols(x, modskv, W["wkvt"], W["wft"], W["bft"], nh, tt=tq)
            k_new = kt.reshape(b, nh, hd, t).transpose(0, 3, 1, 2)
            v_new = vt.reshape(b, nh, hd, t).transpose(0, 3, 1, 2)
            lf_new = lft.transpose(0, 2, 1)
        elif l == n_a:
            k, v, lf_new, kb, vb = _kv_rows(x, modskv, W["wkv"], W["wf"], W["bf"], nh, bb=bb, tt=tt)
            k_new = k.reshape(b, t, nh, hd)
            v_new = v.reshape(b, t, nh, hd)
        x = _ffn(x, mods, ch + 0, W["wg"], W["wu"], W["wd"], (l, 0), W["ln_g"], W["ln_b"],
                 l * 3 + 0, alpha=alpha, bb=bb, tt=tt)
        if l < n_a:
            x, st = _conv(x, mods, ch + 3, conv_prev[l], W["pw1_w"][l], W["pw1_b"][l],
                          W["dw_w"][l], W["dw_b"][l], W["cln_g"][l], W["cln_b"][l],
                          W["pw2_w"][l], W["pw2_b"][l], W["ln_g"], W["ln_b"], l * 3 + 1,
                          alpha=alpha, bb=bb, tt=tt)
            new_conv.append(st)
        else:
            j = l - n_a
            if kv_cache is None:
                x = _attn_prompt(x, mods, ch + 3, W["wq"][j], W["wo"][j], kbt, vbt, cum,
                                 W["ln_g"], W["ln_b"], l * 3 + 1, alpha=alpha, tq=tq)
            else:
                ck, cv, clf = kv_cache
                past = ck.shape[1]
                assert past % LANES == 0 and t <= LANES
                ckt = ck.transpose(0, 2, 3, 1).reshape(b, d, past)
                cvt = cv.transpose(0, 2, 3, 1).reshape(b, d, past)
                lft_new = jnp.pad(lf_new.transpose(0, 2, 1), ((0, 0), (0, 0), (0, LANES - t)))
                cum = _cumsum_heads(jnp.concatenate([clf.transpose(0, 2, 1), lft_new], axis=2), tc=512)
                nk = jnp.pad(kb, ((0, 0), (0, LANES - t), (0, 0)))
                nv = jnp.pad(vb, ((0, 0), (0, LANES - t), (0, 0)))
                x = _attn_sample(x, mods, ch + 3, W["wq"][j], W["wo"][j], ckt, cvt, nk, nv,
                                 cum, W["ln_g"], W["ln_b"], l * 3 + 1, alpha=alpha)
        x = _ffn(x, mods, ch + 6, W["wg"], W["wu"], W["wd"], (l, 1), W["ln_g"], W["ln_b"],
                 l * 3 + 2, alpha=alpha, bb=bb, tt=tt)
    return x, jnp.stack(new_conv), k_new, v_new, lf_new


def kernel(x_prompt, x_sample, cache_conv, cache_k, cache_v, cache_logf, c_prompt, c_sample,
           w_ada, b_ada, ln_g, ln_b, ffn_w_gate, ffn_w_up, ffn_w_down,
           pw1_w, pw1_b, dw_w, dw_b, conv_ln_g, conv_ln_b, pw2_w, pw2_b,
           w_ada_kv, b_ada_kv, w_k, w_v, w_f, b_f, w_q, w_o):
    bp, tp, d = x_prompt.shape
    bs, ts, _ = x_sample.shape
    depth = w_ada.shape[0]
    n_a = pw1_w.shape[0]
    nh = w_f.shape[1]
    assert d % LANES == 0 and d // nh == LANES // HEAD_PAIR and nh <= LANES
    assert ts % 16 == 0

    c_all = jnp.concatenate([c_prompt, c_sample], axis=0)
    mods = _ada(c_all, w_ada, b_ada)
    modskv = _ada(c_all, w_ada_kv[None], b_ada_kv[None])

    W = dict(
        n_layers=depth, n_a=n_a, n_heads=nh,
        wg=ffn_w_gate.astype(BF16), wu=ffn_w_up.astype(BF16), wd=ffn_w_down.astype(BF16),
        ln_g=ln_g.reshape(depth * 3, 1, d), ln_b=ln_b.reshape(depth * 3, 1, d),
        pw1_w=pw1_w.astype(BF16), pw1_b=pw1_b.reshape(n_a, 1, 2 * d),
        dw_w=dw_w, dw_b=dw_b.reshape(n_a, 1, d),
        cln_g=conv_ln_g.reshape(n_a, 1, d), cln_b=conv_ln_b.reshape(n_a, 1, d),
        pw2_w=pw2_w.astype(BF16), pw2_b=pw2_b.reshape(n_a, 1, d),
        wq=w_q.astype(BF16), wo=w_o.astype(BF16),
    )
    W["wkv"] = jnp.concatenate([w_k, w_v], axis=1).astype(BF16)
    W["wf"] = jnp.pad(w_f, ((0, 0), (0, LANES - nh))).astype(BF16)
    W["bf"] = jnp.pad(b_f, (0, LANES - nh)).reshape(1, LANES)
    W["wkvt"], W["wft"], W["bft"] = W["wkv"].T, W["wf"].T, W["bf"].T

    zero_conv = jnp.zeros((n_a, bp, dw_w.shape[1] - 1, d), x_prompt.dtype)
    y_p, conv_p, k_p, v_p, lf_p = _trunk(x_prompt, zero_conv, None, mods[:, :bp], modskv[:, :bp], W, batched=False)
    y_s, conv_s, k_s, v_s, lf_s = _trunk(x_sample, cache_conv, (cache_k, cache_v, cache_logf),
                                         mods[:, bp:], modskv[:, bp:], W, batched=True)
    return (y_p, y_s, conv_p, k_p, v_p, lf_p, conv_s, k_s, v_s, lf_s)
```

```python
import functools

import jax
import jax.numpy as jnp
from jax import lax
from jax.experimental import pallas as pl
from jax.experimental.pallas import tpu as pltpu

F32 = jnp.float32
BF16 = jnp.bfloat16

LN_EPS = 1e-5
LANES = 128
SUBLANES = 8
HEAD_PAIR = 2
CONV_HALO = 32
VMEM_LIMIT_BYTES = 56 * 1024 * 1024
FFN_ROWS = 512
FFN_CHUNK = 512
CONV_TBLOCK = 32
ATTN_GROUP = 8
NEG_BIG = -1e30
LOG2E = 1.4426950408889634


def _cparams(n_grid):
    return pltpu.CompilerParams(
        dimension_semantics=("arbitrary",) * n_grid,
        vmem_limit_bytes=VMEM_LIMIT_BYTES,
    )


def _resident(shape, index_map):
    return pl.BlockSpec(shape, index_map, pipeline_mode=pl.Buffered(1))


def _layer_norm(y, g, b):
    mu = jnp.mean(y, axis=-1, keepdims=True)
    yc = y - mu
    var = jnp.mean(yc * yc, axis=-1, keepdims=True)
    return yc * lax.rsqrt(var + LN_EPS) * g + b


def _dot(a, b):
    return jnp.dot(a, b, preferred_element_type=F32)


def _dot_nt(a, b):
    return lax.dot_general(a, b, (((1,), (1,)), ((), ())), preferred_element_type=F32)


def _split3(x):
    hi = x.astype(BF16)
    r1 = x - hi.astype(F32)
    mid = r1.astype(BF16)
    lo = (r1 - mid.astype(F32)).astype(BF16)
    return hi, mid, lo


def _ada_kernel(c_ref, w_ref, b_ref, o_ref):
    c = c_ref[...]
    ca = (c * jax.nn.sigmoid(c)).astype(BF16)
    o_ref[...] = _dot(ca, w_ref[...].astype(BF16)) + b_ref[...]


def _ada(c, w, b):
    bc, d = c.shape
    nl, _, n = w.shape
    nch = n // d
    out = pl.pallas_call(
        _ada_kernel,
        grid=(nl, nch),
        in_specs=[
            _resident((bc, d), lambda l, j: (0, 0)),
            pl.BlockSpec((None, d, d), lambda l, j: (l, 0, j)),
            pl.BlockSpec((None, 1, d), lambda l, j: (l * nch + j, 0, 0)),
        ],
        out_specs=pl.BlockSpec((None, bc, d), lambda l, j: (l * nch + j, 0, 0)),
        out_shape=jax.ShapeDtypeStruct((nl * nch, bc, d), F32),
        compiler_params=_cparams(2),
        name="ada",
    )(c, w, b.reshape(nl * nch, 1, d))
    return out.reshape(nl * nch, bc, 1, d)


def _mod_spec(bb, d, chunk):
    return pl.BlockSpec((None, bb, 1, d), lambda b, t: (chunk, b, 0, 0))


def _ffn_kernel(x_ref, sh_ref, sc_ref, gt_ref, wg_ref, wu_ref, wd_ref, g_ref, b_ref, o_ref,
                *, alpha):
    bb, tt, d = x_ref.shape
    dff = wg_ref.shape[1]
    sub = min(tt, FFN_ROWS)
    for t0 in range(0, tt, sub):
        x = x_ref[:, t0:t0 + sub, :]
        h = (x * (1.0 + sc_ref[...]) + sh_ref[...]).reshape(bb * sub, d).astype(BF16)
        acc = None
        for c0 in range(0, dff, FFN_CHUNK):
            c1 = min(c0 + FFN_CHUNK, dff)
            g = _dot(h, wg_ref[:, c0:c1])
            u = _dot(h, wu_ref[:, c0:c1])
            a = (g * jax.nn.sigmoid(g) * u).astype(BF16)
            part = _dot(a, wd_ref[c0:c1, :])
            acc = part if acc is None else acc + part
        f = acc.reshape(bb, sub, d)
        y = alpha * x + (0.5 * (1.0 + gt_ref[...])) * f
        o_ref[:, t0:t0 + sub, :] = _layer_norm(y, g_ref[...], b_ref[...])


def _ffn(x, mods, chunk0, wg, wu, wd, widx, ln_g, ln_b, lnidx, *, alpha, bb, tt):
    b, t, d = x.shape
    dff = wg.shape[-1]
    l, i = widx
    kern = functools.partial(_ffn_kernel, alpha=alpha)
    return pl.pallas_call(
        kern,
        grid=(b // bb, t // tt),
        in_specs=[
            pl.BlockSpec((bb, tt, d), lambda bi, ti: (bi, ti, 0)),
            _mod_spec(bb, d, chunk0 + 0),
            _mod_spec(bb, d, chunk0 + 1),
            _mod_spec(bb, d, chunk0 + 2),
            _resident((None, None, d, dff), lambda bi, ti: (l, i, 0, 0)),
            _resident((None, None, d, dff), lambda bi, ti: (l, i, 0, 0)),
            _resident((None, None, dff, d), lambda bi, ti: (l, i, 0, 0)),
            _resident((None, 1, d), lambda bi, ti: (lnidx, 0, 0)),
            _resident((None, 1, d), lambda bi, ti: (lnidx, 0, 0)),
        ],
        out_specs=pl.BlockSpec((bb, tt, d), lambda bi, ti: (bi, ti, 0)),
        out_shape=jax.ShapeDtypeStruct((b, t, d), F32),
        compiler_params=_cparams(2),
        name="ffn",
    )(x, mods, mods, mods, wg, wu, wd, ln_g, ln_b)


def _conv_kernel(x_ref, sh_ref, sc_ref, gt_ref, prev_ref, w1_ref, b1_ref, dw_ref, dwb_ref,
                 cg_ref, cb_ref, w2_ref, b2_ref, g_ref, b_ref, o_ref, st_ref,
                 u_scr, ut_scr, ue_scr, uo_scr, zt_scr,
                 *, alpha, conv_k, tblock):
    bb, tt, d = x_ref.shape
    ti = pl.program_id(1)
    keep = conv_k - 1
    off = CONV_HALO - keep
    nl = d // LANES

    @pl.when(ti == 0)
    def _():
        u_scr[:, 0:off, :] = jnp.zeros((bb, off, d), F32)
        u_scr[:, off:CONV_HALO, :] = prev_ref[...]

    x = x_ref[...]
    h = (x * (1.0 + sc_ref[...]) + sh_ref[...]).reshape(bb * tt, d).astype(BF16)
    u2 = _dot(h, w1_ref[...]) + b1_ref[...]
    u = u2[:, :d] * jax.nn.sigmoid(u2[:, d:])
    u_scr[:, CONV_HALO:CONV_HALO + tt, :] = u.reshape(bb, tt, d)

    n_steps = CONV_HALO + tt
    for bi in range(bb):
        for j in range(nl):
            ut_scr[bi, pl.ds(j, n_steps, stride=nl), :] = u_scr[bi, :, j * LANES:(j + 1) * LANES]
        ue_scr[bi] = ut_scr[bi].astype(BF16)
        uo_scr[bi] = ut_scr[bi, nl:(n_steps - 1) * nl, :].astype(BF16)
    n_blocks = tt // tblock

    def block_body(it, carry):
        bi = it // n_blocks
        base = pl.multiple_of((it % n_blocks) * (tblock * nl), tblock * nl)
        accs = [jnp.zeros((2 * nl, LANES), F32) for _ in range(tblock // 2)]
        for k in range(conv_k):
            w = dw_ref[k].astype(F32)
            shift = off + k
            src, first = (ue_scr, shift) if shift % 2 == 0 else (uo_scr, shift - 1)
            for i in range(tblock // 2):
                u2 = src[bi, pl.ds(base + (2 * i + first) * nl, 2 * nl), :]
                accs[i] = accs[i] + w * u2.astype(F32)
        for i in range(tblock // 2):
            zt_scr[bi, pl.ds(base + 2 * i * nl, 2 * nl), :] = accs[i]
        return carry

    lax.fori_loop(0, bb * n_blocks, block_body, 0)

    st_ref[...] = u_scr[:, tt + off:tt + CONV_HALO, :]
    if tt >= CONV_HALO:
        u_scr[:, 0:CONV_HALO, :] = u_scr[:, tt:tt + CONV_HALO, :]

    z = jnp.stack([
        jnp.concatenate([zt_scr[bi, pl.ds(j, tt, stride=nl), :] for j in range(nl)], axis=1)
        for bi in range(bb)])
    z = z + dwb_ref[...]
    z = _layer_norm(z, cg_ref[...], cb_ref[...])
    z = (z * jax.nn.sigmoid(z)).reshape(bb * tt, d).astype(BF16)
    mix = (_dot(z, w2_ref[...]) + b2_ref[...]).reshape(bb, tt, d)
    y = alpha * x + (1.0 + gt_ref[...]) * mix
    o_ref[...] = _layer_norm(y, g_ref[...], b_ref[...])


def _conv(x, mods, chunk0, prev, w1, b1, dw, dwb, cg, cb, w2, b2, ln_g, ln_b, lnidx,
          *, alpha, bb, tt):
    b, t, d = x.shape
    conv_k = dw.shape[0]
    keep = conv_k - 1
    assert keep <= CONV_HALO and (t == tt or tt >= CONV_HALO)
    nl = d // LANES
    assert nl == SUBLANES
    tblock = min(tt, CONV_TBLOCK)
    assert tblock % 2 == 0 and tt % tblock == 0
    dw_tiles = dw.reshape(conv_k, nl, LANES)
    dw_pairs = jnp.concatenate([dw_tiles, dw_tiles], axis=1).astype(BF16)
    kern = functools.partial(_conv_kernel, alpha=alpha, conv_k=conv_k, tblock=tblock)
    vec = lambda: _resident((1, d), lambda bi, ti: (0, 0))
    return pl.pallas_call(
        kern,
        grid=(b // bb, t // tt),
        in_specs=[
            pl.BlockSpec((bb, tt, d), lambda bi, ti: (bi, ti, 0)),
            _mod_spec(bb, d, chunk0 + 0),
            _mod_spec(bb, d, chunk0 + 1),
            _mod_spec(bb, d, chunk0 + 2),
            pl.BlockSpec((bb, keep, d), lambda bi, ti: (bi, 0, 0)),
            _resident((d, 2 * d), lambda bi, ti: (0, 0)),
            _resident((1, 2 * d), lambda bi, ti: (0, 0)),
            _resident((conv_k, 2 * nl, LANES), lambda bi, ti: (0, 0, 0)),
            vec(), vec(), vec(),
            _resident((d, d), lambda bi, ti: (0, 0)),
            vec(),
            _resident((None, 1, d), lambda bi, ti: (lnidx, 0, 0)),
            _resident((None, 1, d), lambda bi, ti: (lnidx, 0, 0)),
        ],
        out_specs=[
            pl.BlockSpec((bb, tt, d), lambda bi, ti: (bi, ti, 0)),
            pl.BlockSpec((bb, keep, d), lambda bi, ti: (bi, 0, 0)),
        ],
        out_shape=[
            jax.ShapeDtypeStruct((b, t, d), F32),
            jax.ShapeDtypeStruct((b, keep, d), F32),
        ],
        scratch_shapes=[
            pltpu.VMEM((bb, CONV_HALO + tt, d), F32),
            pltpu.VMEM((bb, (CONV_HALO + tt) * nl, LANES), F32),
            pltpu.VMEM((bb, (CONV_HALO + tt) * nl, LANES), BF16),
            pltpu.VMEM((bb, (CONV_HALO + tt - 2) * nl, LANES), BF16),
            pltpu.VMEM((bb, tt * nl, LANES), F32),
        ],
        compiler_params=_cparams(2),
        name="conv",
    )(x, mods, mods, mods, prev, w1, b1, dw_pairs, dwb, cg, cb, w2, b2, ln_g, ln_b)


def _log_sigmoid(f):
    return jnp.minimum(f, 0.0) - jnp.log1p(jnp.exp(-jnp.abs(f)))


def _tri_ones(n):
    row = lax.broadcasted_iota(jnp.int32, (n, n), 0)
    col = lax.broadcasted_iota(jnp.int32, (n, n), 1)
    return jnp.where(row <= col, 1.0, 0.0).astype(BF16)


def _running_sum_lanes(f, tri, carry):
    rows = f.shape[0]
    hi, mid, lo = _split3(f)
    c3 = _dot(jnp.concatenate([hi, mid, lo], axis=0), tri)
    return c3[0:rows] + c3[rows:2 * rows] + c3[2 * rows:3 * rows] + carry


def _kv_rows_kernel(x_ref, sh_ref, sc_ref, wkv_ref, wf_ref, bf_ref, k_ref, v_ref, lf_ref, kb_ref, vb_ref):
    bb, tt, d = x_ref.shape
    nh = lf_ref.shape[-1]
    x = x_ref[...]
    h = (x * (1.0 + sc_ref[...]) + sh_ref[...]).reshape(bb * tt, d).astype(BF16)
    kv = _dot(h, wkv_ref[...])
    k = kv[:, :d].reshape(bb, tt, d)
    v = kv[:, d:].reshape(bb, tt, d)
    k_ref[...] = k
    v_ref[...] = v
    kb_ref[...] = k.astype(BF16)
    vb_ref[...] = v.astype(BF16)
    lf = _log_sigmoid(_dot(h, wf_ref[...]) + bf_ref[...])
    lf_ref[...] = lf[:, :nh].reshape(bb, tt, nh)


def _kv_rows(x, modskv, wkv, wf, bf, nh, *, bb, tt):
    b, t, d = x.shape
    blk = lambda w: pl.BlockSpec((bb, tt, w), lambda bi, ti: (bi, ti, 0))
    return pl.pallas_call(
        _kv_rows_kernel,
        grid=(b // bb, t // tt),
        in_specs=[
            blk(d),
            _mod_spec(bb, d, 0),
            _mod_spec(bb, d, 1),
            _resident((d, 2 * d), lambda bi, ti: (0, 0)),
            _resident((d, LANES), lambda bi, ti: (0, 0)),
            _resident((1, LANES), lambda bi, ti: (0, 0)),
        ],
        out_specs=[blk(d), blk(d), blk(nh), blk(d), blk(d)],
        out_shape=[
            jax.ShapeDtypeStruct((b, t, d), F32),
            jax.ShapeDtypeStruct((b, t, d), F32),
            jax.ShapeDtypeStruct((b, t, nh), F32),
            jax.ShapeDtypeStruct((b, t, d), BF16),
            jax.ShapeDtypeStruct((b, t, d), BF16),
        ],
        compiler_params=_cparams(2),
        name="kv_rows",
    )(x, modskv, modskv, wkv, wf, bf)


def _kv_cols_kernel(x_ref, sh_ref, sc_ref, wt_ref, bft_ref,
                    kt_ref, vt_ref, lft_ref, kbt_ref, vbt_ref, cum_ref, carry_scr, tri_scr):
    _, tt, d = x_ref.shape
    nh = lft_ref.shape[1]
    x = x_ref[0]
    h = (x * (1.0 + sc_ref[0]) + sh_ref[0]).astype(BF16)
    kvft = _dot_nt(wt_ref[...], h)
    kt = kvft[:d]
    vt = kvft[d:2 * d]
    kt_ref[0] = kt
    vt_ref[0] = vt
    kbt_ref[0, 0] = kt.astype(BF16)
    vbt_ref[0, 0] = vt.astype(BF16)
    lft = _log_sigmoid(kvft[2 * d:2 * d + nh] + bft_ref[...])
    lft_ref[0] = lft

    @pl.when(pl.program_id(1) == 0)
    def _():
        carry_scr[...] = jnp.zeros(carry_scr.shape, F32)
        tri_scr[...] = _tri_ones(tt)

    cum = _running_sum_lanes(lft, tri_scr[...], carry_scr[:, 0:1])
    cum_ref[0, 0] = cum
    carry_scr[...] = jnp.broadcast_to(cum[:, tt - 1:tt], carry_scr.shape)


def _kv_cols(x, modskv, wt, bft, nh, *, tt):
    b, t, d = x.shape
    nt = t // tt
    assert nh % 16 == 0
    cols = lambda rows: pl.BlockSpec((1, rows, tt), lambda bi, ti: (bi, 0, ti))
    chunk = lambda rows: pl.BlockSpec((1, 1, rows, tt), lambda bi, ti: (bi, ti, 0, 0))
    return pl.pallas_call(
        _kv_cols_kernel,
        grid=(b, nt),
        in_specs=[
            pl.BlockSpec((1, tt, d), lambda bi, ti: (bi, ti, 0)),
            _mod_spec(1, d, 0),
            _mod_spec(1, d, 1),
            _resident((2 * d + LANES, d), lambda bi, ti: (0, 0)),
            _resident((nh, 1), lambda bi, ti: (0, 0)),
        ],
        out_specs=[cols(d), cols(d), cols(nh), chunk(d), chunk(d), chunk(nh)],
        out_shape=[
            jax.ShapeDtypeStruct((b, d, t), F32),
            jax.ShapeDtypeStruct((b, d, t), F32),
            jax.ShapeDtypeStruct((b, nh, t), F32),
            jax.ShapeDtypeStruct((b, nt, d, tt), BF16),
            jax.ShapeDtypeStruct((b, nt, d, tt), BF16),
            jax.ShapeDtypeStruct((b, nt, nh, tt), F32),
        ],
        scratch_shapes=[pltpu.VMEM((nh, LANES), F32), pltpu.VMEM((tt, tt), BF16)],
        compiler_params=_cparams(2),
        name="kv_cols",
    )(x, modskv, modskv, wt, bft)


def _cumsum_kernel(lf_ref, o_ref, *, tc):
    _, nh, t = lf_ref.shape
    tri = _tri_ones(tc)
    carry = jnp.zeros((nh, 1), F32)
    for c0 in range(0, t, tc):
        n = min(tc, t - c0)
        cum = _running_sum_lanes(lf_ref[0, :, c0:c0 + n], tri[0:n, 0:n], carry)
        o_ref[0, :, c0:c0 + n] = cum
        carry = cum[:, n - 1:n]


def _cumsum_heads(lft, *, tc):
    b, nh, t = lft.shape
    assert nh % 16 == 0 and t % LANES == 0
    return pl.pallas_call(
        functools.partial(_cumsum_kernel, tc=tc),
        grid=(b,),
        in_specs=[pl.BlockSpec((1, nh, t), lambda bi: (bi, 0, 0))],
        out_specs=pl.BlockSpec((1, nh, t), lambda bi: (bi, 0, 0)),
        out_shape=jax.ShapeDtypeStruct((b, nh, t), F32),
        compiler_params=_cparams(1),
        name="cumsum",
    )(lft)


def _pair_queries(q2):
    lane = lax.broadcasted_iota(jnp.int32, q2.shape, 1)
    first = lane < (LANES // HEAD_PAIR)
    zero = jnp.zeros_like(q2)
    return jnp.concatenate([jnp.where(first, q2, zero), jnp.where(first, zero, q2)], axis=0)


def _pair_causal(tq, tk):
    row = lax.broadcasted_iota(jnp.int32, (2 * tq, tk), 0)
    col = lax.broadcasted_iota(jnp.int32, (2 * tq, tk), 1)
    return col <= jnp.where(row >= tq, row - tq, row)


def _pair_bias(b0, b1, tq, mult):
    n = b0.shape[-1]
    return jnp.concatenate([jnp.broadcast_to(b0 * (-mult), (tq, n)),
                            jnp.broadcast_to(b1 * (-mult), (tq, n))], axis=0)


def _attn_prompt_kernel(x_ref, sh_ref, sc_ref, gt_ref, wq_ref, wo_ref, k_ref, v_ref, cum_ref,
                        g_ref, b_ref, o_ref, qq_scr, m_scr, acc_scr, att_scr,
                        *, alpha, scale):
    _, tq, d = x_ref.shape
    tk = tq
    n_pair = d // LANES
    qi = pl.program_id(1)
    x = x_ref[0]
    h = (x * (1.0 + sc_ref[0]) + sh_ref[0]).astype(BF16)
    q = (_dot(h, wq_ref[...]) * (scale * LOG2E)).astype(BF16)
    half = tq // 2
    lane = lax.broadcasted_iota(jnp.int32, (tq, LANES), 1)
    first = lane < (LANES // HEAD_PAIR)
    for pair in range(n_pair):
        q2 = q[:, pair * LANES:(pair + 1) * LANES]
        qa = jnp.where(first, q2, jnp.zeros_like(q2))
        qb = jnp.where(first, jnp.zeros_like(q2), q2)
        qq_scr[pair] = jnp.concatenate([qa[0:half], qb[0:half], qa[half:tq], qb[half:tq]], axis=0)
    n_grp = m_scr.shape[0]

    def update(g, pair, kj, r0, nrows, c0, ncols, mask):
        chans = pl.ds(pair * LANES, LANES)
        rows = pl.ds(r0, nrows)
        s = _dot(qq_scr[pair, rows, :], k_ref[0, kj, chans, c0:c0 + ncols])
        b0 = cum_ref[0, kj, pl.ds(HEAD_PAIR * pair, 1), c0:c0 + ncols] * (-LOG2E)
        b1 = cum_ref[0, kj, pl.ds(HEAD_PAIR * pair + 1, 1), c0:c0 + ncols] * (-LOG2E)
        row = lax.broadcasted_iota(jnp.int32, (nrows, ncols), 0)
        s = s + jnp.where((row & half) == 0, b0, b1)
        if mask is not None:
            s = jnp.where(mask, s, NEG_BIG)
        m_old = m_scr[g, rows, :]
        m_new = jnp.maximum(m_old, jnp.broadcast_to(jnp.max(s, axis=-1, keepdims=True), m_old.shape))
        a = jnp.exp2(m_old - m_new)
        p = jnp.exp2(s - jnp.concatenate([m_new] * (ncols // LANES), axis=1))
        v_ext = jnp.concatenate([v_ref[0, kj, chans, c0:c0 + ncols], jnp.ones((LANES, ncols), BF16)], axis=0)
        acc_scr[g, rows, :] = (jnp.concatenate([a, a], axis=1) * acc_scr[g, rows, :]
                               + _dot_nt(p.astype(BF16), v_ext))
        m_scr[g, rows, :] = m_new

    row_a = lax.broadcasted_iota(jnp.int32, (2 * tq, half), 0)
    col_a = lax.broadcasted_iota(jnp.int32, (2 * tq, half), 1)
    mask_a = col_a <= jnp.where(row_a >= tq, half, row_a & (half - 1))
    row_b = lax.broadcasted_iota(jnp.int32, (tq, half), 0)
    col_b = lax.broadcasted_iota(jnp.int32, (tq, half), 1)
    mask_b = col_b <= (row_b & (half - 1))

    for pair0 in range(0, n_pair, n_grp):
        m_scr[...] = jnp.full(m_scr.shape, NEG_BIG, F32)
        acc_scr[...] = jnp.zeros(acc_scr.shape, F32)

        def full_tile(kj, carry, pair0=pair0):
            for g in range(n_grp):
                update(g, pair0 + g, kj, 0, 2 * tq, 0, tk, None)
            return carry

        lax.fori_loop(0, qi, full_tile, 0)
        for g in range(n_grp):
            update(g, pair0 + g, qi, 0, 2 * tq, 0, half, mask_a)
        for g in range(n_grp):
            update(g, pair0 + g, qi, tq, tq, half, half, mask_b)
        for g in range(n_grp):
            o = acc_scr[g, :, 0:LANES] / acc_scr[g, :, LANES:2 * LANES]
            oa = jnp.concatenate([o[0:half], o[tq:tq + half]], axis=0)
            ob = jnp.concatenate([o[half:tq], o[tq + half:2 * tq]], axis=0)
            att_scr[:, (pair0 + g) * LANES:(pair0 + g + 1) * LANES] = jnp.where(first, oa, ob).astype(BF16)

    mix = _dot(att_scr[...], wo_ref[...])
    y = alpha * x + (1.0 + gt_ref[0]) * mix
    o_ref[0] = _layer_norm(y, g_ref[...], b_ref[...])


def _attn_prompt(x, mods, chunk0, wq, wo, kbt, vbt, cum, ln_g, ln_b, lnidx, *, alpha, tq):
    b, t, d = x.shape
    nt, nh = cum.shape[1], cum.shape[2]
    assert kbt.shape == (b, nt, d, tq) and nt * tq == t
    scale = float(d // nh) ** -0.5
    grp = min(ATTN_GROUP, d // LANES)
    assert (d // LANES) % grp == 0
    assert tq % (2 * LANES) == 0 and (tq // 2) & (tq // 2 - 1) == 0
    kern = functools.partial(_attn_prompt_kernel, alpha=alpha, scale=scale)
    return pl.pallas_call(
        kern,
        grid=(b, t // tq),
        in_specs=[
            pl.BlockSpec((1, tq, d), lambda bi, qi: (bi, qi, 0)),
            _mod_spec(1, d, chunk0 + 0),
            _mod_spec(1, d, chunk0 + 1),
            _mod_spec(1, d, chunk0 + 2),
            _resident((d, d), lambda bi, qi: (0, 0)),
            _resident((d, d), lambda bi, qi: (0, 0)),
            pl.BlockSpec((1, nt, d, tq), lambda bi, qi: (bi, 0, 0, 0)),
            pl.BlockSpec((1, nt, d, tq), lambda bi, qi: (bi, 0, 0, 0)),
            pl.BlockSpec((1, nt, nh, tq), lambda bi, qi: (bi, 0, 0, 0)),
            _resident((None, 1, d), lambda bi, qi: (lnidx, 0, 0)),
            _resident((None, 1, d), lambda bi, qi: (lnidx, 0, 0)),
        ],
        out_specs=pl.BlockSpec((1, tq, d), lambda bi, qi: (bi, qi, 0)),
        out_shape=jax.ShapeDtypeStruct((b, t, d), F32),
        scratch_shapes=[
            pltpu.VMEM((d // LANES, 2 * tq, LANES), BF16),
            pltpu.VMEM((grp, 2 * tq, LANES), F32),
            pltpu.VMEM((grp, 2 * tq, 2 * LANES), F32),
            pltpu.VMEM((tq, d), BF16),
        ],
        compiler_params=_cparams(2),
        name="attn_prompt",
    )(x, mods, mods, mods, wq, wo, kbt, vbt, cum, ln_g, ln_b)


def _attn_sample_kernel(x_ref, sh_ref, sc_ref, gt_ref, wq_ref, wo_ref, ck_ref, cv_ref, nk_ref,
                        nv_ref, cum_ref, g_ref, b_ref, o_ref, att_scr, *, alpha, scale):
    _, tq, d = x_ref.shape
    past = ck_ref.shape[2]
    tn = nk_ref.shape[1]
    x = x_ref[0]
    h = (x * (1.0 + sc_ref[0]) + sh_ref[0]).astype(BF16)
    q = (_dot(h, wq_ref[...]) * scale).astype(BF16)
    causal = _pair_causal(tq, tn)
    lane = lax.broadcasted_iota(jnp.int32, (tq, LANES), 1)

    def pair_bias(pair, lo, n):
        h0 = HEAD_PAIR * pair
        return _pair_bias(cum_ref[0, h0:h0 + 1, lo:lo + n], cum_ref[0, h0 + 1:h0 + 2, lo:lo + n], tq, 1.0)

    for pair in range(d // LANES):
        lanes = pl.ds(pair * LANES, LANES)
        qq = _pair_queries(q[:, pair * LANES:(pair + 1) * LANES])
        s_p = _dot(qq, ck_ref[0, lanes, :].astype(BF16)) + pair_bias(pair, 0, past)
        s_n = _dot_nt(qq, nk_ref[0, :, lanes]) + pair_bias(pair, past, tn)
        s_n = jnp.where(causal, s_n, NEG_BIG)
        m = jnp.maximum(jnp.max(s_p, axis=-1, keepdims=True), jnp.max(s_n, axis=-1, keepdims=True))
        p_p = jnp.exp(s_p - m)
        p_n = jnp.exp(s_n - m)
        l = jnp.sum(p_p, axis=-1, keepdims=True) + jnp.sum(p_n, axis=-1, keepdims=True)
        o = (_dot_nt(p_p.astype(BF16), cv_ref[0, lanes, :].astype(BF16))
             + _dot(p_n.astype(BF16), nv_ref[0, :, lanes]))
        o = o / l
        att_scr[:, lanes] = jnp.where(lane < (LANES // HEAD_PAIR), o[0:tq], o[tq:2 * tq]).astype(BF16)

    mix = _dot(att_scr[...], wo_ref[...])
    y = alpha * x + (1.0 + gt_ref[0]) * mix
    o_ref[0] = _layer_norm(y, g_ref[...], b_ref[...])


def _attn_sample(x, mods, chunk0, wq, wo, ckt, cvt, nk, nv, cum, ln_g, ln_b, lnidx, *, alpha):
    b, tq, d = x.shape
    past = ckt.shape[2]
    tn = nk.shape[1]
    nh = cum.shape[1]
    scale = float(d // nh) ** -0.5
    kern = functools.partial(_attn_sample_kernel, alpha=alpha, scale=scale)
    per_b = lambda shape: pl.BlockSpec(shape, lambda bi: (bi,) + (0,) * (len(shape) - 1))
    mod = lambda c: pl.BlockSpec((None, 1, 1, d), lambda bi: (c, bi, 0, 0))
    return pl.pallas_call(
        kern,
        grid=(b,),
        in_specs=[
            per_b((1, tq, d)),
            mod(chunk0 + 0), mod(chunk0 + 1), mod(chunk0 + 2),
            _resident((d, d), lambda bi: (0, 0)),
            _resident((d, d), lambda bi: (0, 0)),
            per_b((1, d, past)), per_b((1, d, past)),
            per_b((1, tn, d)), per_b((1, tn, d)),
            per_b((1, nh, cum.shape[2])),
            _resident((None, 1, d), lambda bi: (lnidx, 0, 0)),
            _resident((None, 1, d), lambda bi: (lnidx, 0, 0)),
        ],
        out_specs=per_b((1, tq, d)),
        out_shape=jax.ShapeDtypeStruct((b, tq, d), F32),
        scratch_shapes=[pltpu.VMEM((tq, d), BF16)],
        compiler_params=_cparams(1),
        name="attn_sample",
    )(x, mods, mods, mods, wq, wo, ckt, cvt, nk, nv, cum, ln_g, ln_b)


def _tiles(t):
    tile = 512 if t % 512 == 0 else t
    return tile, tile, (2 * FFN_ROWS if t % (2 * FFN_ROWS) == 0 else tile)


def _trunk(x, conv_prev, kv_cache, mods, modskv, W, *, batched):
    b, t, d = x.shape
    depth = W["n_layers"]
    n_a = W["n_a"]
    nh = W["n_heads"]
    alpha = float((2 * depth) ** 0.25)
    tt, tq, tf = _tiles(t)
    bb = b if batched else 1
    tt = t if batched else tt
    tf = t if batched else tf
    new_conv = []
    k_new = v_new = lf_new = None
    hd = d // nh
    for l in range(depth):
        ch = l * 9
        if l == n_a and kv_cache is None:
            kt, vt, lft, kbt, vbt, cum = _kv_cols(x, modskv, W["wkvft"], W["bft"], nh, tt=tq)
            k_new = kt.reshape(b, nh, hd, t).transpose(0, 3, 1, 2)
            v_new = vt.reshape(b, nh, hd, t).transpose(0, 3, 1, 2)
            lf_new = lft.transpose(0, 2, 1)
        elif l == n_a:
            k, v, lf_new, kb, vb = _kv_rows(x, modskv, W["wkv"], W["wf"], W["bf"], nh, bb=bb, tt=tt)
            k_new = k.reshape(b, t, nh, hd)
            v_new = v.reshape(b, t, nh, hd)
        x = _ffn(x, mods, ch + 0, W["wg"], W["wu"], W["wd"], (l, 0), W["ln_g"], W["ln_b"],
                 l * 3 + 0, alpha=alpha, bb=bb, tt=tf)
        if l < n_a:
            x, st = _conv(x, mods, ch + 3, conv_prev[l], W["pw1_w"][l], W["pw1_b"][l],
                          W["dw_w"][l], W["dw_b"][l], W["cln_g"][l], W["cln_b"][l],
                          W["pw2_w"][l], W["pw2_b"][l], W["ln_g"], W["ln_b"], l * 3 + 1,
                          alpha=alpha, bb=bb, tt=tt)
            new_conv.append(st)
        else:
            j = l - n_a
            if kv_cache is None:
                x = _attn_prompt(x, mods, ch + 3, W["wq"][j], W["wo"][j], kbt, vbt, cum,
                                 W["ln_g"], W["ln_b"], l * 3 + 1, alpha=alpha, tq=tq)
            else:
                ck, cv, clf = kv_cache
                past = ck.shape[1]
                assert past % LANES == 0 and t <= LANES
                ckt = ck.transpose(0, 2, 3, 1).reshape(b, d, past)
                cvt = cv.transpose(0, 2, 3, 1).reshape(b, d, past)
                lft_new = jnp.pad(lf_new.transpose(0, 2, 1), ((0, 0), (0, 0), (0, LANES - t)))
                cum = _cumsum_heads(jnp.concatenate([clf.transpose(0, 2, 1), lft_new], axis=2), tc=512)
                nk = jnp.pad(kb, ((0, 0), (0, LANES - t), (0, 0)))
                nv = jnp.pad(vb, ((0, 0), (0, LANES - t), (0, 0)))
                x = _attn_sample(x, mods, ch + 3, W["wq"][j], W["wo"][j], ckt, cvt, nk, nv,
                                 cum, W["ln_g"], W["ln_b"], l * 3 + 1, alpha=alpha)
        x = _ffn(x, mods, ch + 6, W["wg"], W["wu"], W["wd"], (l, 1), W["ln_g"], W["ln_b"],
                 l * 3 + 2, alpha=alpha, bb=bb, tt=tf)
    return x, jnp.stack(new_conv), k_new, v_new, lf_new


def kernel(x_prompt, x_sample, cache_conv, cache_k, cache_v, cache_logf, c_prompt, c_sample,
           w_ada, b_ada, ln_g, ln_b, ffn_w_gate, ffn_w_up, ffn_w_down,
           pw1_w, pw1_b, dw_w, dw_b, conv_ln_g, conv_ln_b, pw2_w, pw2_b,
           w_ada_kv, b_ada_kv, w_k, w_v, w_f, b_f, w_q, w_o):
    bp, tp, d = x_prompt.shape
    bs, ts, _ = x_sample.shape
    depth = w_ada.shape[0]
    n_a = pw1_w.shape[0]
    nh = w_f.shape[1]
    assert d % LANES == 0 and d // nh == LANES // HEAD_PAIR and nh <= LANES
    assert ts % 16 == 0

    c_all = jnp.concatenate([c_prompt, c_sample], axis=0)
    mods = _ada(c_all, w_ada, b_ada)
    modskv = _ada(c_all, w_ada_kv[None], b_ada_kv[None])

    W = dict(
        n_layers=depth, n_a=n_a, n_heads=nh,
        wg=ffn_w_gate.astype(BF16), wu=ffn_w_up.astype(BF16), wd=ffn_w_down.astype(BF16),
        ln_g=ln_g.reshape(depth * 3, 1, d), ln_b=ln_b.reshape(depth * 3, 1, d),
        pw1_w=pw1_w.astype(BF16), pw1_b=pw1_b.reshape(n_a, 1, 2 * d),
        dw_w=dw_w, dw_b=dw_b.reshape(n_a, 1, d),
        cln_g=conv_ln_g.reshape(n_a, 1, d), cln_b=conv_ln_b.reshape(n_a, 1, d),
        pw2_w=pw2_w.astype(BF16), pw2_b=pw2_b.reshape(n_a, 1, d),
        wq=w_q.astype(BF16), wo=w_o.astype(BF16),
    )
    W["wkv"] = jnp.concatenate([w_k, w_v], axis=1).astype(BF16)
    W["wf"] = jnp.pad(w_f, ((0, 0), (0, LANES - nh))).astype(BF16)
    W["bf"] = jnp.pad(b_f, (0, LANES - nh)).reshape(1, LANES)
    W["wkvft"] = jnp.concatenate([W["wkv"], W["wf"]], axis=1).T
    W["bft"] = b_f.reshape(nh, 1)

    zero_conv = jnp.zeros((n_a, bp, dw_w.shape[1] - 1, d), x_prompt.dtype)
    y_p, conv_p, k_p, v_p, lf_p = _trunk(x_prompt, zero_conv, None, mods[:, :bp], modskv[:, :bp], W, batched=False)
    y_s, conv_s, k_s, v_s, lf_s = _trunk(x_sample, cache_conv, (cache_k, cache_v, cache_logf),
                                         mods[:, bp:], modskv[:, bp:], W, batched=True)
    return (y_p, y_s, conv_p, k_p, v_p, lf_p, conv_s, k_s, v_s, lf_s)
```

```python
import functools

import jax
import jax.numpy as jnp
from jax import lax
from jax.experimental import pallas as pl
from jax.experimental.pallas import tpu as pltpu

F32 = jnp.float32
BF16 = jnp.bfloat16

LN_EPS = 1e-5
LANES = 128
SUBLANES = 8
HEAD_PAIR = 2
CONV_HALO = 32
VMEM_LIMIT_BYTES = 56 * 1024 * 1024
FFN_ROWS = 512
FFN_CHUNK = 512
CONV_TBLOCK = 32
ATTN_GROUP = 8
NEG_BIG = -1e30
LOG2E = 1.4426950408889634


def _cparams(n_grid):
    return pltpu.CompilerParams(
        dimension_semantics=("arbitrary",) * n_grid,
        vmem_limit_bytes=VMEM_LIMIT_BYTES,
    )


def _resident(shape, index_map):
    return pl.BlockSpec(shape, index_map, pipeline_mode=pl.Buffered(1))


def _layer_norm(y, g, b):
    mu = jnp.mean(y, axis=-1, keepdims=True)
    yc = y - mu
    var = jnp.mean(yc * yc, axis=-1, keepdims=True)
    return yc * lax.rsqrt(var + LN_EPS) * g + b


def _dot(a, b):
    return jnp.dot(a, b, preferred_element_type=F32)


def _dot_nt(a, b):
    return lax.dot_general(a, b, (((1,), (1,)), ((), ())), preferred_element_type=F32)


def _split3(x):
    hi = x.astype(BF16)
    r1 = x - hi.astype(F32)
    mid = r1.astype(BF16)
    lo = (r1 - mid.astype(F32)).astype(BF16)
    return hi, mid, lo


def _ada_kernel(c_ref, w_ref, b_ref, o_ref):
    c = c_ref[...]
    ca = (c * jax.nn.sigmoid(c)).astype(BF16)
    o_ref[...] = _dot(ca, w_ref[...].astype(BF16)) + b_ref[...]


def _ada(c, w, b):
    bc, d = c.shape
    nl, _, n = w.shape
    nch = n // d
    out = pl.pallas_call(
        _ada_kernel,
        grid=(nl, nch),
        in_specs=[
            _resident((bc, d), lambda l, j: (0, 0)),
            pl.BlockSpec((None, d, d), lambda l, j: (l, 0, j)),
            pl.BlockSpec((None, 1, d), lambda l, j: (l * nch + j, 0, 0)),
        ],
        out_specs=pl.BlockSpec((None, bc, d), lambda l, j: (l * nch + j, 0, 0)),
        out_shape=jax.ShapeDtypeStruct((nl * nch, bc, d), F32),
        compiler_params=_cparams(2),
        name="ada",
    )(c, w, b.reshape(nl * nch, 1, d))
    return out.reshape(nl * nch, bc, 1, d)


def _mod_spec(bb, d, chunk):
    return pl.BlockSpec((None, bb, 1, d), lambda b, t: (chunk, b, 0, 0))


def _ffn_kernel(x_ref, sh_ref, sc_ref, gt_ref, wg_ref, wu_ref, wd_ref, g_ref, b_ref, o_ref,
                *, alpha):
    bb, tt, d = x_ref.shape
    dff = wg_ref.shape[1]
    sub = min(tt, FFN_ROWS)
    for t0 in range(0, tt, sub):
        x = x_ref[:, t0:t0 + sub, :]
        h = (x * (1.0 + sc_ref[...]) + sh_ref[...]).reshape(bb * sub, d).astype(BF16)
        acc = None
        for c0 in range(0, dff, FFN_CHUNK):
            c1 = min(c0 + FFN_CHUNK, dff)
            g = _dot(h, wg_ref[:, c0:c1])
            u = _dot(h, wu_ref[:, c0:c1])
            a = (g * jax.nn.sigmoid(g) * u).astype(BF16)
            part = _dot(a, wd_ref[c0:c1, :])
            acc = part if acc is None else acc + part
        f = acc.reshape(bb, sub, d)
        y = alpha * x + (0.5 * (1.0 + gt_ref[...])) * f
        o_ref[:, t0:t0 + sub, :] = _layer_norm(y, g_ref[...], b_ref[...])


def _ffn(x, mods, chunk0, wg, wu, wd, widx, ln_g, ln_b, lnidx, *, alpha, bb, tt):
    b, t, d = x.shape
    dff = wg.shape[-1]
    l, i = widx
    kern = functools.partial(_ffn_kernel, alpha=alpha)
    return pl.pallas_call(
        kern,
        grid=(b // bb, t // tt),
        in_specs=[
            pl.BlockSpec((bb, tt, d), lambda bi, ti: (bi, ti, 0)),
            _mod_spec(bb, d, chunk0 + 0),
            _mod_spec(bb, d, chunk0 + 1),
            _mod_spec(bb, d, chunk0 + 2),
            _resident((None, None, d, dff), lambda bi, ti: (l, i, 0, 0)),
            _resident((None, None, d, dff), lambda bi, ti: (l, i, 0, 0)),
            _resident((None, None, dff, d), lambda bi, ti: (l, i, 0, 0)),
            _resident((None, 1, d), lambda bi, ti: (lnidx, 0, 0)),
            _resident((None, 1, d), lambda bi, ti: (lnidx, 0, 0)),
        ],
        out_specs=pl.BlockSpec((bb, tt, d), lambda bi, ti: (bi, ti, 0)),
        out_shape=jax.ShapeDtypeStruct((b, t, d), F32),
        compiler_params=_cparams(2),
        name="ffn",
    )(x, mods, mods, mods, wg, wu, wd, ln_g, ln_b)


def _conv_kernel(x_ref, sh_ref, sc_ref, gt_ref, prev_ref, w1_ref, b1_ref, dw_ref, dwb_ref,
                 cg_ref, cb_ref, w2_ref, b2_ref, g_ref, b_ref, o_ref, st_ref,
                 u_scr, ut_scr, ue_scr, uo_scr, zt_scr,
                 *, alpha, conv_k, tblock):
    bb, tt, d = x_ref.shape
    ti = pl.program_id(1)
    keep = conv_k - 1
    off = CONV_HALO - keep
    nl = d // LANES

    @pl.when(ti == 0)
    def _():
        u_scr[:, 0:off, :] = jnp.zeros((bb, off, d), F32)
        u_scr[:, off:CONV_HALO, :] = prev_ref[...]

    x = x_ref[...]
    h = (x * (1.0 + sc_ref[...]) + sh_ref[...]).reshape(bb * tt, d).astype(BF16)
    u2 = _dot(h, w1_ref[...]) + b1_ref[...]
    u = u2[:, :d] * jax.nn.sigmoid(u2[:, d:])
    u_scr[:, CONV_HALO:CONV_HALO + tt, :] = u.reshape(bb, tt, d)

    n_steps = CONV_HALO + tt
    for bi in range(bb):
        for j in range(nl):
            ut_scr[bi, pl.ds(j, n_steps, stride=nl), :] = u_scr[bi, :, j * LANES:(j + 1) * LANES]
        ue_scr[bi] = ut_scr[bi].astype(BF16)
        uo_scr[bi] = ut_scr[bi, nl:(n_steps - 1) * nl, :].astype(BF16)
    st_ref[...] = u_scr[:, tt + off:tt + CONV_HALO, :]

    parts = 2 if (bb == 1 and tt % (2 * tblock) == 0) else 1
    prows = tt // parts
    for part in range(parts):
        for bi in range(bb):
            for blk in range(prows // tblock):
                base = (part * prows + blk * tblock) * nl
                accs = [jnp.zeros((2 * nl, LANES), F32) for _ in range(tblock // 2)]
                for k in range(conv_k):
                    w = dw_ref[k].astype(F32)
                    shift = off + k
                    src, first = (ue_scr, shift) if shift % 2 == 0 else (uo_scr, shift - 1)
                    for i in range(tblock // 2):
                        u2 = src[bi, pl.ds(base + (2 * i + first) * nl, 2 * nl), :]
                        accs[i] = accs[i] + w * u2.astype(F32)
                for i in range(tblock // 2):
                    zt_scr[bi, pl.ds(base + 2 * i * nl, 2 * nl), :] = accs[i]
        z = jnp.stack([
            jnp.concatenate([zt_scr[bi, pl.ds(part * prows * nl + j, prows, stride=nl), :] for j in range(nl)], axis=1)
            for bi in range(bb)])
        z = z + dwb_ref[...]
        z = _layer_norm(z, cg_ref[...], cb_ref[...])
        z = (z * jax.nn.sigmoid(z)).reshape(bb * prows, d).astype(BF16)
        mix = (_dot(z, w2_ref[...]) + b2_ref[...]).reshape(bb, prows, d)
        rows = slice(part * prows, (part + 1) * prows)
        y = alpha * x_ref[:, rows, :] + (1.0 + gt_ref[...]) * mix
        o_ref[:, rows, :] = _layer_norm(y, g_ref[...], b_ref[...])

    if tt >= CONV_HALO:
        u_scr[:, 0:CONV_HALO, :] = u_scr[:, tt:tt + CONV_HALO, :]


def _conv(x, mods, chunk0, prev, w1, b1, dw, dwb, cg, cb, w2, b2, ln_g, ln_b, lnidx,
          *, alpha, bb, tt):
    b, t, d = x.shape
    conv_k = dw.shape[0]
    keep = conv_k - 1
    assert keep <= CONV_HALO and (t == tt or tt >= CONV_HALO)
    nl = d // LANES
    assert nl == SUBLANES
    tblock = min(tt, CONV_TBLOCK)
    assert tblock % 2 == 0 and tt % tblock == 0
    dw_tiles = dw.reshape(conv_k, nl, LANES)
    dw_pairs = jnp.concatenate([dw_tiles, dw_tiles], axis=1).astype(BF16)
    kern = functools.partial(_conv_kernel, alpha=alpha, conv_k=conv_k, tblock=tblock)
    vec = lambda: _resident((1, d), lambda bi, ti: (0, 0))
    return pl.pallas_call(
        kern,
        grid=(b // bb, t // tt),
        in_specs=[
            pl.BlockSpec((bb, tt, d), lambda bi, ti: (bi, ti, 0)),
            _mod_spec(bb, d, chunk0 + 0),
            _mod_spec(bb, d, chunk0 + 1),
            _mod_spec(bb, d, chunk0 + 2),
            pl.BlockSpec((bb, keep, d), lambda bi, ti: (bi, 0, 0)),
            _resident((d, 2 * d), lambda bi, ti: (0, 0)),
            _resident((1, 2 * d), lambda bi, ti: (0, 0)),
            _resident((conv_k, 2 * nl, LANES), lambda bi, ti: (0, 0, 0)),
            vec(), vec(), vec(),
            _resident((d, d), lambda bi, ti: (0, 0)),
            vec(),
            _resident((None, 1, d), lambda bi, ti: (lnidx, 0, 0)),
            _resident((None, 1, d), lambda bi, ti: (lnidx, 0, 0)),
        ],
        out_specs=[
            pl.BlockSpec((bb, tt, d), lambda bi, ti: (bi, ti, 0)),
            pl.BlockSpec((bb, keep, d), lambda bi, ti: (bi, 0, 0)),
        ],
        out_shape=[
            jax.ShapeDtypeStruct((b, t, d), F32),
            jax.ShapeDtypeStruct((b, keep, d), F32),
        ],
        scratch_shapes=[
            pltpu.VMEM((bb, CONV_HALO + tt, d), F32),
            pltpu.VMEM((bb, (CONV_HALO + tt) * nl, LANES), F32),
            pltpu.VMEM((bb, (CONV_HALO + tt) * nl, LANES), BF16),
            pltpu.VMEM((bb, (CONV_HALO + tt - 2) * nl, LANES), BF16),
            pltpu.VMEM((bb, tt * nl, LANES), F32),
        ],
        compiler_params=_cparams(2),
        name="conv",
    )(x, mods, mods, mods, prev, w1, b1, dw_pairs, dwb, cg, cb, w2, b2, ln_g, ln_b)


def _log_sigmoid(f):
    return jnp.minimum(f, 0.0) - jnp.log1p(jnp.exp(-jnp.abs(f)))


def _tri_ones(n):
    row = lax.broadcasted_iota(jnp.int32, (n, n), 0)
    col = lax.broadcasted_iota(jnp.int32, (n, n), 1)
    return jnp.where(row <= col, 1.0, 0.0).astype(BF16)


def _running_sum_lanes(f, tri, carry):
    rows = f.shape[0]
    hi, mid, lo = _split3(f)
    c3 = _dot(jnp.concatenate([hi, mid, lo], axis=0), tri)
    return c3[0:rows] + c3[rows:2 * rows] + c3[2 * rows:3 * rows] + carry


def _kv_rows_kernel(x_ref, sh_ref, sc_ref, wkv_ref, wf_ref, bf_ref, k_ref, v_ref, lf_ref, kb_ref, vb_ref):
    bb, tt, d = x_ref.shape
    nh = lf_ref.shape[-1]
    x = x_ref[...]
    h = (x * (1.0 + sc_ref[...]) + sh_ref[...]).reshape(bb * tt, d).astype(BF16)
    kv = _dot(h, wkv_ref[...])
    k = kv[:, :d].reshape(bb, tt, d)
    v = kv[:, d:].reshape(bb, tt, d)
    k_ref[...] = k
    v_ref[...] = v
    kb_ref[...] = k.astype(BF16)
    vb_ref[...] = v.astype(BF16)
    lf = _log_sigmoid(_dot(h, wf_ref[...]) + bf_ref[...])
    lf_ref[...] = lf[:, :nh].reshape(bb, tt, nh)


def _kv_rows(x, modskv, wkv, wf, bf, nh, *, bb, tt):
    b, t, d = x.shape
    blk = lambda w: pl.BlockSpec((bb, tt, w), lambda bi, ti: (bi, ti, 0))
    return pl.pallas_call(
        _kv_rows_kernel,
        grid=(b // bb, t // tt),
        in_specs=[
            blk(d),
            _mod_spec(bb, d, 0),
            _mod_spec(bb, d, 1),
            _resident((d, 2 * d), lambda bi, ti: (0, 0)),
            _resident((d, LANES), lambda bi, ti: (0, 0)),
            _resident((1, LANES), lambda bi, ti: (0, 0)),
        ],
        out_specs=[blk(d), blk(d), blk(nh), blk(d), blk(d)],
        out_shape=[
            jax.ShapeDtypeStruct((b, t, d), F32),
            jax.ShapeDtypeStruct((b, t, d), F32),
            jax.ShapeDtypeStruct((b, t, nh), F32),
            jax.ShapeDtypeStruct((b, t, d), BF16),
            jax.ShapeDtypeStruct((b, t, d), BF16),
        ],
        compiler_params=_cparams(2),
        name="kv_rows",
    )(x, modskv, modskv, wkv, wf, bf)


def _kv_cols_kernel(x_ref, sh_ref, sc_ref, wt_ref, bft_ref,
                    kt_ref, vt_ref, lft_ref, kbt_ref, vbt_ref, cum_ref, carry_scr, tri_scr):
    _, tt, d = x_ref.shape
    nh = lft_ref.shape[1]
    x = x_ref[0]
    h = (x * (1.0 + sc_ref[0]) + sh_ref[0]).astype(BF16)
    kvft = _dot_nt(wt_ref[...], h)
    kt = kvft[:d]
    vt = kvft[d:2 * d]
    kt_ref[0] = kt
    vt_ref[0] = vt
    kbt_ref[0, 0] = kt.astype(BF16)
    vbt_ref[0, 0] = vt.astype(BF16)
    lft = _log_sigmoid(kvft[2 * d:2 * d + nh] + bft_ref[...])
    lft_ref[0] = lft

    @pl.when(pl.program_id(1) == 0)
    def _():
        carry_scr[...] = jnp.zeros(carry_scr.shape, F32)
        tri_scr[...] = _tri_ones(tt)

    cum = _running_sum_lanes(lft, tri_scr[...], carry_scr[:, 0:1])
    cum_ref[0, 0] = cum
    carry_scr[...] = jnp.broadcast_to(cum[:, tt - 1:tt], carry_scr.shape)


def _kv_cols(x, modskv, wt, bft, nh, *, tt):
    b, t, d = x.shape
    nt = t // tt
    assert nh % 16 == 0
    cols = lambda rows: pl.BlockSpec((1, rows, tt), lambda bi, ti: (bi, 0, ti))
    chunk = lambda rows: pl.BlockSpec((1, 1, rows, tt), lambda bi, ti: (bi, ti, 0, 0))
    return pl.pallas_call(
        _kv_cols_kernel,
        grid=(b, nt),
        in_specs=[
            pl.BlockSpec((1, tt, d), lambda bi, ti: (bi, ti, 0)),
            _mod_spec(1, d, 0),
            _mod_spec(1, d, 1),
            _resident((2 * d + LANES, d), lambda bi, ti: (0, 0)),
            _resident((nh, 1), lambda bi, ti: (0, 0)),
        ],
        out_specs=[cols(d), cols(d), cols(nh), chunk(d), chunk(d), chunk(nh)],
        out_shape=[
            jax.ShapeDtypeStruct((b, d, t), F32),
            jax.ShapeDtypeStruct((b, d, t), F32),
            jax.ShapeDtypeStruct((b, nh, t), F32),
            jax.ShapeDtypeStruct((b, nt, d, tt), BF16),
            jax.ShapeDtypeStruct((b, nt, d, tt), BF16),
            jax.ShapeDtypeStruct((b, nt, nh, tt), F32),
        ],
        scratch_shapes=[pltpu.VMEM((nh, LANES), F32), pltpu.VMEM((tt, tt), BF16)],
        compiler_params=_cparams(2),
        name="kv_cols",
    )(x, modskv, modskv, wt, bft)


def _cumsum_kernel(lf_ref, o_ref, *, tc):
    _, nh, t = lf_ref.shape
    tri = _tri_ones(tc)
    carry = jnp.zeros((nh, 1), F32)
    for c0 in range(0, t, tc):
        n = min(tc, t - c0)
        cum = _running_sum_lanes(lf_ref[0, :, c0:c0 + n], tri[0:n, 0:n], carry)
        o_ref[0, :, c0:c0 + n] = cum
        carry = cum[:, n - 1:n]


def _cumsum_heads(lft, *, tc):
    b, nh, t = lft.shape
    assert nh % 16 == 0 and t % LANES == 0
    return pl.pallas_call(
        functools.partial(_cumsum_kernel, tc=tc),
        grid=(b,),
        in_specs=[pl.BlockSpec((1, nh, t), lambda bi: (bi, 0, 0))],
        out_specs=pl.BlockSpec((1, nh, t), lambda bi: (bi, 0, 0)),
        out_shape=jax.ShapeDtypeStruct((b, nh, t), F32),
        compiler_params=_cparams(1),
        name="cumsum",
    )(lft)


def _pair_queries(q2):
    lane = lax.broadcasted_iota(jnp.int32, q2.shape, 1)
    first = lane < (LANES // HEAD_PAIR)
    zero = jnp.zeros_like(q2)
    return jnp.concatenate([jnp.where(first, q2, zero), jnp.where(first, zero, q2)], axis=0)


def _pair_causal(tq, tk):
    row = lax.broadcasted_iota(jnp.int32, (2 * tq, tk), 0)
    col = lax.broadcasted_iota(jnp.int32, (2 * tq, tk), 1)
    return col <= jnp.where(row >= tq, row - tq, row)


def _pair_bias(b0, b1, tq, mult):
    n = b0.shape[-1]
    return jnp.concatenate([jnp.broadcast_to(b0 * (-mult), (tq, n)),
                            jnp.broadcast_to(b1 * (-mult), (tq, n))], axis=0)


def _attn_prompt_kernel(x_ref, sh_ref, sc_ref, gt_ref, wq_ref, wo_ref, k_ref, v_ref, cum_ref,
                        g_ref, b_ref, o_ref, qq_scr, m_scr, acc_scr, att_scr,
                        *, alpha, scale):
    _, tq, d = x_ref.shape
    tk = tq
    n_pair = d // LANES
    qi = pl.program_id(1)
    x = x_ref[0]
    h = (x * (1.0 + sc_ref[0]) + sh_ref[0]).astype(BF16)
    q = (_dot(h, wq_ref[...]) * (scale * LOG2E)).astype(BF16)
    half = tq // 2
    lane = lax.broadcasted_iota(jnp.int32, (tq, LANES), 1)
    first = lane < (LANES // HEAD_PAIR)
    for pair in range(n_pair):
        q2 = q[:, pair * LANES:(pair + 1) * LANES]
        qa = jnp.where(first, q2, jnp.zeros_like(q2))
        qb = jnp.where(first, jnp.zeros_like(q2), q2)
        qq_scr[pair] = jnp.concatenate([qa[0:half], qb[0:half], qa[half:tq], qb[half:tq]], axis=0)
    n_grp = m_scr.shape[0]

    def update(g, pair, kj, r0, nrows, c0, ncols, mask):
        chans = pl.ds(pair * LANES, LANES)
        rows = pl.ds(r0, nrows)
        s = _dot(qq_scr[pair, rows, :], k_ref[0, kj, chans, c0:c0 + ncols])
        b0 = cum_ref[0, kj, pl.ds(HEAD_PAIR * pair, 1), c0:c0 + ncols] * (-LOG2E)
        b1 = cum_ref[0, kj, pl.ds(HEAD_PAIR * pair + 1, 1), c0:c0 + ncols] * (-LOG2E)
        row = lax.broadcasted_iota(jnp.int32, (nrows, ncols), 0)
        s = s + jnp.where((row & half) == 0, b0, b1)
        if mask is not None:
            s = jnp.where(mask, s, NEG_BIG)
        m_old = m_scr[g, rows, :]
        m_new = jnp.maximum(m_old, jnp.broadcast_to(jnp.max(s, axis=-1, keepdims=True), m_old.shape))
        a = jnp.exp2(m_old - m_new)
        p = jnp.exp2(s - jnp.concatenate([m_new] * (ncols // LANES), axis=1))
        v_ext = jnp.concatenate([v_ref[0, kj, chans, c0:c0 + ncols], jnp.ones((LANES, ncols), BF16)], axis=0)
        acc_scr[g, rows, :] = (jnp.concatenate([a, a], axis=1) * acc_scr[g, rows, :]
                               + _dot_nt(p.astype(BF16), v_ext))
        m_scr[g, rows, :] = m_new

    row_a = lax.broadcasted_iota(jnp.int32, (2 * tq, half), 0)
    col_a = lax.broadcasted_iota(jnp.int32, (2 * tq, half), 1)
    mask_a = col_a <= jnp.where(row_a >= tq, half, row_a & (half - 1))
    row_b = lax.broadcasted_iota(jnp.int32, (tq, half), 0)
    col_b = lax.broadcasted_iota(jnp.int32, (tq, half), 1)
    mask_b = col_b <= (row_b & (half - 1))

    for pair0 in range(0, n_pair, n_grp):
        m_scr[...] = jnp.full(m_scr.shape, NEG_BIG, F32)
        acc_scr[...] = jnp.zeros(acc_scr.shape, F32)

        def full_tile(kj, carry, pair0=pair0):
            for g in range(n_grp):
                update(g, pair0 + g, kj, 0, 2 * tq, 0, tk, None)
            return carry

        lax.fori_loop(0, qi, full_tile, 0)
        for g in range(n_grp):
            update(g, pair0 + g, qi, 0, 2 * tq, 0, half, mask_a)
        for g in range(n_grp):
            update(g, pair0 + g, qi, tq, tq, half, half, mask_b)
        for g in range(n_grp):
            o = acc_scr[g, :, 0:LANES] / acc_scr[g, :, LANES:2 * LANES]
            oa = jnp.concatenate([o[0:half], o[tq:tq + half]], axis=0)
            ob = jnp.concatenate([o[half:tq], o[tq + half:2 * tq]], axis=0)
            att_scr[:, (pair0 + g) * LANES:(pair0 + g + 1) * LANES] = jnp.where(first, oa, ob).astype(BF16)

    mix = _dot(att_scr[...], wo_ref[...])
    y = alpha * x + (1.0 + gt_ref[0]) * mix
    o_ref[0] = _layer_norm(y, g_ref[...], b_ref[...])


def _attn_prompt(x, mods, chunk0, wq, wo, kbt, vbt, cum, ln_g, ln_b, lnidx, *, alpha, tq):
    b, t, d = x.shape
    nt, nh = cum.shape[1], cum.shape[2]
    assert kbt.shape == (b, nt, d, tq) and nt * tq == t
    scale = float(d // nh) ** -0.5
    grp = min(ATTN_GROUP, d // LANES)
    assert (d // LANES) % grp == 0
    assert tq % (2 * LANES) == 0 and (tq // 2) & (tq // 2 - 1) == 0
    kern = functools.partial(_attn_prompt_kernel, alpha=alpha, scale=scale)
    return pl.pallas_call(
        kern,
        grid=(b, t // tq),
        in_specs=[
            pl.BlockSpec((1, tq, d), lambda bi, qi: (bi, qi, 0)),
            _mod_spec(1, d, chunk0 + 0),
            _mod_spec(1, d, chunk0 + 1),
            _mod_spec(1, d, chunk0 + 2),
            _resident((d, d), lambda bi, qi: (0, 0)),
            _resident((d, d), lambda bi, qi: (0, 0)),
            pl.BlockSpec((1, nt, d, tq), lambda bi, qi: (bi, 0, 0, 0)),
            pl.BlockSpec((1, nt, d, tq), lambda bi, qi: (bi, 0, 0, 0)),
            pl.BlockSpec((1, nt, nh, tq), lambda bi, qi: (bi, 0, 0, 0)),
            _resident((None, 1, d), lambda bi, qi: (lnidx, 0, 0)),
            _resident((None, 1, d), lambda bi, qi: (lnidx, 0, 0)),
        ],
        out_specs=pl.BlockSpec((1, tq, d), lambda bi, qi: (bi, qi, 0)),
        out_shape=jax.ShapeDtypeStruct((b, t, d), F32),
        scratch_shapes=[
            pltpu.VMEM((d // LANES, 2 * tq, LANES), BF16),
            pltpu.VMEM((grp, 2 * tq, LANES), F32),
            pltpu.VMEM((grp, 2 * tq, 2 * LANES), F32),
            pltpu.VMEM((tq, d), BF16),
        ],
        compiler_params=_cparams(2),
        name="attn_prompt",
    )(x, mods, mods, mods, wq, wo, kbt, vbt, cum, ln_g, ln_b)


def _attn_sample_kernel(x_ref, sh_ref, sc_ref, gt_ref, wq_ref, wo_ref, ck_ref, cv_ref, nk_ref,
                        nv_ref, cum_ref, g_ref, b_ref, o_ref, att_scr, *, alpha, scale):
    _, tq, d = x_ref.shape
    past = ck_ref.shape[2]
    tn = nk_ref.shape[1]
    x = x_ref[0]
    h = (x * (1.0 + sc_ref[0]) + sh_ref[0]).astype(BF16)
    q = (_dot(h, wq_ref[...]) * scale).astype(BF16)
    causal = _pair_causal(tq, tn)
    lane = lax.broadcasted_iota(jnp.int32, (tq, LANES), 1)

    def pair_bias(pair, lo, n):
        h0 = HEAD_PAIR * pair
        return _pair_bias(cum_ref[0, h0:h0 + 1, lo:lo + n], cum_ref[0, h0 + 1:h0 + 2, lo:lo + n], tq, 1.0)

    for pair in range(d // LANES):
        lanes = pl.ds(pair * LANES, LANES)
        qq = _pair_queries(q[:, pair * LANES:(pair + 1) * LANES])
        s_p = _dot(qq, ck_ref[0, lanes, :].astype(BF16)) + pair_bias(pair, 0, past)
        s_n = _dot_nt(qq, nk_ref[0, :, lanes]) + pair_bias(pair, past, tn)
        s_n = jnp.where(causal, s_n, NEG_BIG)
        m = jnp.maximum(jnp.max(s_p, axis=-1, keepdims=True), jnp.max(s_n, axis=-1, keepdims=True))
        p_p = jnp.exp(s_p - m)
        p_n = jnp.exp(s_n - m)
        l = jnp.sum(p_p, axis=-1, keepdims=True) + jnp.sum(p_n, axis=-1, keepdims=True)
        o = (_dot_nt(p_p.astype(BF16), cv_ref[0, lanes, :].astype(BF16))
             + _dot(p_n.astype(BF16), nv_ref[0, :, lanes]))
        o = o / l
        att_scr[:, lanes] = jnp.where(lane < (LANES // HEAD_PAIR), o[0:tq], o[tq:2 * tq]).astype(BF16)

    mix = _dot(att_scr[...], wo_ref[...])
    y = alpha * x + (1.0 + gt_ref[0]) * mix
    o_ref[0] = _layer_norm(y, g_ref[...], b_ref[...])


def _attn_sample(x, mods, chunk0, wq, wo, ckt, cvt, nk, nv, cum, ln_g, ln_b, lnidx, *, alpha):
    b, tq, d = x.shape
    past = ckt.shape[2]
    tn = nk.shape[1]
    nh = cum.shape[1]
    scale = float(d // nh) ** -0.5
    kern = functools.partial(_attn_sample_kernel, alpha=alpha, scale=scale)
    per_b = lambda shape: pl.BlockSpec(shape, lambda bi: (bi,) + (0,) * (len(shape) - 1))
    mod = lambda c: pl.BlockSpec((None, 1, 1, d), lambda bi: (c, bi, 0, 0))
    return pl.pallas_call(
        kern,
        grid=(b,),
        in_specs=[
            per_b((1, tq, d)),
            mod(chunk0 + 0), mod(chunk0 + 1), mod(chunk0 + 2),
            _resident((d, d), lambda bi: (0, 0)),
            _resident((d, d), lambda bi: (0, 0)),
            per_b((1, d, past)), per_b((1, d, past)),
            per_b((1, tn, d)), per_b((1, tn, d)),
            per_b((1, nh, cum.shape[2])),
            _resident((None, 1, d), lambda bi: (lnidx, 0, 0)),
            _resident((None, 1, d), lambda bi: (lnidx, 0, 0)),
        ],
        out_specs=per_b((1, tq, d)),
        out_shape=jax.ShapeDtypeStruct((b, tq, d), F32),
        scratch_shapes=[pltpu.VMEM((tq, d), BF16)],
        compiler_params=_cparams(1),
        name="attn_sample",
    )(x, mods, mods, mods, wq, wo, ckt, cvt, nk, nv, cum, ln_g, ln_b)


def _tiles(t):
    tile = 512 if t % 512 == 0 else t
    return tile, tile, (2 * FFN_ROWS if t % (2 * FFN_ROWS) == 0 else tile)


def _trunk(x, conv_prev, kv_cache, mods, modskv, W, *, batched):
    b, t, d = x.shape
    depth = W["n_layers"]
    n_a = W["n_a"]
    nh = W["n_heads"]
    alpha = float((2 * depth) ** 0.25)
    tt, tq, tf = _tiles(t)
    bb = b if batched else 1
    tt = t if batched else tt
    tf = t if batched else tf
    new_conv = []
    k_new = v_new = lf_new = None
    hd = d // nh
    for l in range(depth):
        ch = l * 9
        if l == n_a and kv_cache is None:
            kt, vt, lft, kbt, vbt, cum = _kv_cols(x, modskv, W["wkvft"], W["bft"], nh, tt=tq)
            k_new = kt.reshape(b, nh, hd, t).transpose(0, 3, 1, 2)
            v_new = vt.reshape(b, nh, hd, t).transpose(0, 3, 1, 2)
            lf_new = lft.transpose(0, 2, 1)
        elif l == n_a:
            k, v, lf_new, kb, vb = _kv_rows(x, modskv, W["wkv"], W["wf"], W["bf"], nh, bb=bb, tt=tt)
            k_new = k.reshape(b, t, nh, hd)
            v_new = v.reshape(b, t, nh, hd)
        x = _ffn(x, mods, ch + 0, W["wg"], W["wu"], W["wd"], (l, 0), W["ln_g"], W["ln_b"],
                 l * 3 + 0, alpha=alpha, bb=bb, tt=tf)
        if l < n_a:
            x, st = _conv(x, mods, ch + 3, conv_prev[l], W["pw1_w"][l], W["pw1_b"][l],
                          W["dw_w"][l], W["dw_b"][l], W["cln_g"][l], W["cln_b"][l],
                          W["pw2_w"][l], W["pw2_b"][l], W["ln_g"], W["ln_b"], l * 3 + 1,
                          alpha=alpha, bb=bb, tt=tt)
            new_conv.append(st)
        else:
            j = l - n_a
            if kv_cache is None:
                x = _attn_prompt(x, mods, ch + 3, W["wq"][j], W["wo"][j], kbt, vbt, cum,
                                 W["ln_g"], W["ln_b"], l * 3 + 1, alpha=alpha, tq=tq)
            else:
                ck, cv, clf = kv_cache
                past = ck.shape[1]
                assert past % LANES == 0 and t <= LANES
                ckt = ck.transpose(0, 2, 3, 1).reshape(b, d, past)
                cvt = cv.transpose(0, 2, 3, 1).reshape(b, d, past)
                lft_new = jnp.pad(lf_new.transpose(0, 2, 1), ((0, 0), (0, 0), (0, LANES - t)))
                cum = _cumsum_heads(jnp.concatenate([clf.transpose(0, 2, 1), lft_new], axis=2), tc=512)
                nk = jnp.pad(kb, ((0, 0), (0, LANES - t), (0, 0)))
                nv = jnp.pad(vb, ((0, 0), (0, LANES - t), (0, 0)))
                x = _attn_sample(x, mods, ch + 3, W["wq"][j], W["wo"][j], ckt, cvt, nk, nv,
                                 cum, W["ln_g"], W["ln_b"], l * 3 + 1, alpha=alpha)
        x = _ffn(x, mods, ch + 6, W["wg"], W["wu"], W["wd"], (l, 1), W["ln_g"], W["ln_b"],
                 l * 3 + 2, alpha=alpha, bb=bb, tt=tf)
    return x, jnp.stack(new_conv), k_new, v_new, lf_new


def kernel(x_prompt, x_sample, cache_conv, cache_k, cache_v, cache_logf, c_prompt, c_sample,
           w_ada, b_ada, ln_g, ln_b, ffn_w_gate, ffn_w_up, ffn_w_down,
           pw1_w, pw1_b, dw_w, dw_b, conv_ln_g, conv_ln_b, pw2_w, pw2_b,
           w_ada_kv, b_ada_kv, w_k, w_v, w_f, b_f, w_q, w_o):
    bp, tp, d = x_prompt.shape
    bs, ts, _ = x_sample.shape
    depth = w_ada.shape[0]
    n_a = pw1_w.shape[0]
    nh = w_f.shape[1]
    assert d % LANES == 0 and d // nh == LANES // HEAD_PAIR and nh <= LANES
    assert ts % 16 == 0

    c_all = jnp.concatenate([c_prompt, c_sample], axis=0)
    mods = _ada(c_all, w_ada, b_ada)
    modskv = _ada(c_all, w_ada_kv[None], b_ada_kv[None])

    W = dict(
        n_layers=depth, n_a=n_a, n_heads=nh,
        wg=ffn_w_gate.astype(BF16), wu=ffn_w_up.astype(BF16), wd=ffn_w_down.astype(BF16),
        ln_g=ln_g.reshape(depth * 3, 1, d), ln_b=ln_b.reshape(depth * 3, 1, d),
        pw1_w=pw1_w.astype(BF16), pw1_b=pw1_b.reshape(n_a, 1, 2 * d),
        dw_w=dw_w, dw_b=dw_b.reshape(n_a, 1, d),
        cln_g=conv_ln_g.reshape(n_a, 1, d), cln_b=conv_ln_b.reshape(n_a, 1, d),
        pw2_w=pw2_w.astype(BF16), pw2_b=pw2_b.reshape(n_a, 1, d),
        wq=w_q.astype(BF16), wo=w_o.astype(BF16),
    )
    W["wkv"] = jnp.concatenate([w_k, w_v], axis=1).astype(BF16)
    W["wf"] = jnp.pad(w_f, ((0, 0), (0, LANES - nh))).astype(BF16)
    W["bf"] = jnp.pad(b_f, (0, LANES - nh)).reshape(1, LANES)
    W["wkvft"] = jnp.concatenate([W["wkv"], W["wf"]], axis=1).T
    W["bft"] = b_f.reshape(nh, 1)

    zero_conv = jnp.zeros((n_a, bp, dw_w.shape[1] - 1, d), x_prompt.dtype)
    y_p, conv_p, k_p, v_p, lf_p = _trunk(x_prompt, zero_conv, None, mods[:, :bp], modskv[:, :bp], W, batched=False)
    y_s, conv_s, k_s, v_s, lf_s = _trunk(x_sample, cache_conv, (cache_k, cache_v, cache_logf),
                                         mods[:, bp:], modskv[:, bp:], W, batched=True)
    return (y_p, y_s, conv_p, k_p, v_p, lf_p, conv_s, k_s, v_s, lf_s)
```
